```python
import math
import jax
import jax.numpy as jnp
from jax import lax
import numpy as np

D_MODEL = 1024
BATCH = 2
SEQ = 16384
DEPTH = 4

CHUNK = 64
N_MIXERS = 2
N_ATTN_LAYERS = (DEPTH + 1) // 2
N_HGRN_LAYERS = DEPTH // 2

A_HEADS = 8
A_HEAD_DIM = D_MODEL // (2 * A_HEADS)
A_QK_WIDTH = A_HEADS * 2 * A_HEAD_DIM
A_V_WIDTH = A_HEADS * 2 * A_HEAD_DIM
Q_BLOCK = 128
ROT_DIM = A_HEAD_DIM // 4
ROPE_THETA = 500000.0

H_EXPAND = 128
H_HEADS = D_MODEL // H_EXPAND
H_WIDTH = H_HEADS * H_EXPAND
H_VDIM = H_WIDTH // H_HEADS

N_EXPERTS = 64
N_GROUPS = 8
TOPK_GROUP = 4
TOP_K = 8
EXPERT_FF = D_MODEL // 4
ROUTED_SCALE = 2.5
MOE_BLOCK = 256

ALPHA = (2 * DEPTH) ** 0.25
BETA = (8 * DEPTH) ** -0.25
LN_EPS = 1e-5

kernel_name = "hybrid_diffattn_hgrn2_moe_deepnorm"


def layer_norm(x, g, b):
    xf = x.astype(jnp.float32)
    mu = jnp.mean(xf, axis=-1, keepdims=True)
    var = jnp.mean(jnp.square(xf - mu), axis=-1, keepdims=True)
    y = (xf - mu) * lax.rsqrt(var + LN_EPS) * g.astype(jnp.float32) + b.astype(jnp.float32)
    return y.astype(x.dtype)


def rms_norm(x, g):
    xf = x.astype(jnp.float32)
    y = xf * lax.rsqrt(jnp.mean(jnp.square(xf), axis=-1, keepdims=True) + LN_EPS)
    return y * g.astype(jnp.float32)


def rotary_tables(seq):
    pos = jnp.arange(seq, dtype=jnp.float32)
    inv_freq = ROPE_THETA ** (-jnp.arange(0, ROT_DIM, 2, dtype=jnp.float32) / ROT_DIM)
    ang = pos[:, None] * inv_freq[None, :]
    return jnp.cos(ang), jnp.sin(ang)


def partial_rotary(x, cos, sin):
    half = ROT_DIM // 2
    c = cos[None, :, None, :].astype(x.dtype)
    s = sin[None, :, None, :].astype(x.dtype)
    x1, x2, rest = x[..., :half], x[..., half:ROT_DIM], x[..., ROT_DIM:]
    return jnp.concatenate([x1 * c - x2 * s, x2 * c + x1 * s, rest], axis=-1)


def diff_attention(x, w_in, w_out, lam_params, subln_g, lambda_init, cos, sin):
    B, S, _ = x.shape
    nqb = S // Q_BLOCK
    q, k, v = jnp.split(x @ w_in, [A_QK_WIDTH, 2 * A_QK_WIDTH], axis=-1)
    q = partial_rotary(q.reshape(B, S, A_HEADS * 2, A_HEAD_DIM), cos, sin)
    k = partial_rotary(k.reshape(B, S, A_HEADS * 2, A_HEAD_DIM), cos, sin)
    q = q.reshape(B, nqb, Q_BLOCK, A_HEADS, 2, A_HEAD_DIM).transpose(1, 4, 0, 3, 2, 5)
    k = k.reshape(B, S, A_HEADS, 2, A_HEAD_DIM).transpose(3, 0, 2, 1, 4)
    v = v.reshape(B, S, A_HEADS, 2 * A_HEAD_DIM).transpose(0, 2, 1, 3)
    lp = lam_params.astype(jnp.float32)
    lam = jnp.exp(jnp.sum(lp[0] * lp[1])) - jnp.exp(jnp.sum(lp[2] * lp[3])) + lambda_init
    scale = A_HEAD_DIM ** -0.5
    key_chunk = jnp.arange(S) // CHUNK

    def attend(args):
        qb, bi = args
        q_chunk = (bi * Q_BLOCK + jnp.arange(Q_BLOCK)) // CHUNK
        mask = key_chunk[None, :] <= q_chunk[:, None]
        s = jnp.einsum("mbhqd,mbhkd->mbhqk", qb, k).astype(jnp.float32) * scale
        p = jax.nn.softmax(jnp.where(mask, s, -jnp.inf), axis=-1)
        a = (p[0] - lam * p[1]).astype(v.dtype)
        return jnp.einsum("bhqk,bhkd->bhqd", a, v)

    o = lax.map(attend, (q, jnp.arange(nqb)))
    o = o.transpose(1, 0, 3, 2, 4).reshape(B, S, A_HEADS, 2 * A_HEAD_DIM)
    o = rms_norm(o, subln_g) * (1.0 - lambda_init)
    return o.astype(x.dtype).reshape(B, S, A_V_WIDTH) @ w_out


def hgrn2(x, w_in, w_out, lower_bound, norm_g):
    B, S, _ = x.shape
    nc = S // CHUNK
    q, f, i, g = jnp.split(x @ w_in, 4, axis=-1)
    forget = lower_bound + (1.0 - lower_bound) * jax.nn.sigmoid(f.astype(jnp.float32))
    log_f = jnp.log(forget)
    k_in = 1.0 - forget
    q = jax.nn.silu(q)

    def to_chunks(t):
        d = t.shape[-1] // H_HEADS
        return t.astype(jnp.float32).reshape(B, nc, CHUNK, H_HEADS, d).transpose(1, 0, 3, 2, 4)

    causal = jnp.tril(jnp.ones((CHUNK, CHUNK), dtype=bool))[:, :, None]

    def step(state, xs):
        qc, kc, vc, gc = xs
        b = jnp.cumsum(gc, axis=2)
        diff = b[:, :, :, None, :] - b[:, :, None, :, :]
        decay = jnp.exp(jnp.where(causal, diff, -jnp.inf))
        scores = jnp.einsum("bhtd,bhtsd,bhsd->bhts", qc, decay, kc)
        b_last = b[:, :, -1:, :]
        o = jnp.einsum("bhts,bhse->bhte", scores, vc) + jnp.einsum("bhtd,bhde->bhte", qc * jnp.exp(b), state)
        state = jnp.exp(b_last[:, :, 0, :, None]) * state + jnp.einsum("bhsd,bhse->bhde", kc * jnp.exp(b_last - b), vc)
        return state, o

    state0 = jnp.zeros((B, H_HEADS, H_EXPAND, H_VDIM), jnp.float32)
    _, o = lax.scan(step, state0, (to_chunks(q), to_chunks(k_in), to_chunks(i), to_chunks(log_f)))
    o = o.transpose(1, 0, 3, 2, 4).reshape(B, S, H_HEADS, H_VDIM)
    o = rms_norm(o, norm_g).reshape(B, S, H_WIDTH) * jax.nn.silu(g.astype(jnp.float32))
    return o.astype(x.dtype) @ w_out


def swiglu(x, w_gate_up, w_down):
    gate, up = jnp.split(x @ w_gate_up, 2, axis=-1)
    return (jax.nn.silu(gate) * up) @ w_down


def route(xf, w_router, bias):
    T = xf.shape[0]
    scores = jax.nn.sigmoid(xf.astype(jnp.float32) @ w_router.astype(jnp.float32))
    choice = (scores + bias.astype(jnp.float32)).reshape(T, N_GROUPS, N_EXPERTS // N_GROUPS)
    group_score = jnp.sum(lax.top_k(choice, 2)[0], axis=-1)
    _, top_groups = lax.top_k(group_score, TOPK_GROUP)
    group_mask = jnp.any(jax.nn.one_hot(top_groups, N_GROUPS, dtype=jnp.bool_), axis=1)
    choice = jnp.where(group_mask[:, :, None], choice, -jnp.inf).reshape(T, N_EXPERTS)
    _, idx = lax.top_k(choice, TOP_K)
    w = jnp.take_along_axis(scores, idx, axis=-1)
    w = w / (jnp.sum(w, axis=-1, keepdims=True) + 1e-20) * ROUTED_SCALE
    return idx, w


def routed_experts(xf, idx, w, w_gate_up, w_down):
    T, D = xf.shape
    n_assign = T * TOP_K
    n_blocks = -(-n_assign // MOE_BLOCK) + N_EXPERTS
    flat_e = idx.reshape(-1)
    order = jnp.argsort(flat_e)
    sorted_e = flat_e[order]
    counts = jnp.zeros((N_EXPERTS,), jnp.int32).at[flat_e].add(1)
    padded = (counts + MOE_BLOCK - 1) // MOE_BLOCK * MOE_BLOCK
    pad_end = jnp.cumsum(padded)
    pad_start = pad_end - padded
    start = jnp.cumsum(counts) - counts
    dest = pad_start[sorted_e] + jnp.arange(n_assign, dtype=jnp.int32) - start[sorted_e]
    rows = jnp.full((n_blocks * MOE_BLOCK,), T, jnp.int32).at[dest].set((order // TOP_K).astype(jnp.int32))
    gates = jnp.zeros((n_blocks * MOE_BLOCK,), xf.dtype).at[dest].set(w.reshape(-1)[order].astype(xf.dtype))
    block_expert = jnp.minimum(jnp.searchsorted(pad_end // MOE_BLOCK, jnp.arange(n_blocks), side="right"), N_EXPERTS - 1)
    x_pad = jnp.concatenate([xf, jnp.zeros((1, D), xf.dtype)], axis=0)

    def block(acc, xs):
        r, gt, e = xs
        y = swiglu(x_pad[r], w_gate_up[e], w_down[e])
        return acc.at[r].add(gt[:, None] * y), None

    acc, _ = lax.scan(block, jnp.zeros((T + 1, D), xf.dtype),
                      (rows.reshape(n_blocks, MOE_BLOCK), gates.reshape(n_blocks, MOE_BLOCK), block_expert))
    return acc[:T]


def moe(x, w_router, bias, w_gate_up, w_down, ws_gate_up, ws_down):
    B, S, D = x.shape
    xf = x.reshape(B * S, D)
    idx, w = route(xf, w_router, bias)
    y = routed_experts(xf, idx, w, w_gate_up, w_down) + swiglu(xf, ws_gate_up, ws_down)
    return y.reshape(B, S, D)


def setup_inputs(seed: int = 0) -> dict:
    key = jax.random.key(seed)
    ks = jax.random.split(key, 24)

    def nrm(k, shape, scale):
        return jax.random.normal(k, shape, jnp.float32) * scale

    x = nrm(ks[0], (BATCH, SEQ, D_MODEL), 1.0)
    attn_w_in = jnp.concatenate([
        nrm(ks[1], (N_ATTN_LAYERS, D_MODEL, 2 * A_QK_WIDTH), D_MODEL ** -0.5),
        nrm(ks[2], (N_ATTN_LAYERS, D_MODEL, A_V_WIDTH), BETA * D_MODEL ** -0.5)], axis=-1)
    attn_w_out = nrm(ks[3], (N_ATTN_LAYERS, A_V_WIDTH, D_MODEL), BETA * A_V_WIDTH ** -0.5)
    attn_lambda = nrm(ks[4], (N_ATTN_LAYERS, 4, A_HEAD_DIM), 0.1)
    attn_subln = 1.0 + nrm(ks[5], (N_ATTN_LAYERS, 2 * A_HEAD_DIM), 0.02)
    hgrn_w_in = jnp.concatenate([
        nrm(ks[6], (N_HGRN_LAYERS, D_MODEL, 2 * H_WIDTH), D_MODEL ** -0.5),
        nrm(ks[7], (N_HGRN_LAYERS, D_MODEL, H_WIDTH), BETA * D_MODEL ** -0.5),
        nrm(ks[8], (N_HGRN_LAYERS, D_MODEL, H_WIDTH), D_MODEL ** -0.5)], axis=-1)
    hgrn_w_out = nrm(ks[9], (N_HGRN_LAYERS, H_WIDTH, D_MODEL), BETA * H_WIDTH ** -0.5)
    hgrn_lower_bounds = nrm(ks[10], (DEPTH, H_WIDTH), 0.1)
    hgrn_norm = 1.0 + nrm(ks[11], (N_HGRN_LAYERS, H_VDIM), 0.02)
    moe_router = nrm(ks[12], (DEPTH, D_MODEL, N_EXPERTS), D_MODEL ** -0.5)
    moe_router_bias = nrm(ks[13], (DEPTH, N_EXPERTS), 0.01)
    moe_w_gate_up = nrm(ks[14], (DEPTH, N_EXPERTS, D_MODEL, 2 * EXPERT_FF), D_MODEL ** -0.5)
    moe_w_down = nrm(ks[15], (DEPTH, N_EXPERTS, EXPERT_FF, D_MODEL), BETA * EXPERT_FF ** -0.5)
    shared_w_gate_up = nrm(ks[16], (DEPTH, D_MODEL, 2 * EXPERT_FF), D_MODEL ** -0.5)
    shared_w_down = nrm(ks[17], (DEPTH, EXPERT_FF, D_MODEL), BETA * EXPERT_FF ** -0.5)
    ln_gain = 1.0 + nrm(ks[18], (DEPTH, 2, D_MODEL), 0.02)
    ln_bias = nrm(ks[19], (DEPTH, 2, D_MODEL), 0.02)
    return {"x": x, "attn_w_in": attn_w_in, "attn_w_out": attn_w_out, "attn_lambda": attn_lambda,
            "attn_subln": attn_subln, "hgrn_w_in": hgrn_w_in, "hgrn_w_out": hgrn_w_out,
            "hgrn_lower_bounds": hgrn_lower_bounds, "hgrn_norm": hgrn_norm, "moe_router": moe_router,
            "moe_router_bias": moe_router_bias, "moe_w_gate_up": moe_w_gate_up, "moe_w_down": moe_w_down,
            "shared_w_gate_up": shared_w_gate_up, "shared_w_down": shared_w_down,
            "ln_gain": ln_gain, "ln_bias": ln_bias}


def reference(x, attn_w_in, attn_w_out, attn_lambda, attn_subln, hgrn_w_in, hgrn_w_out,
              hgrn_lower_bounds, hgrn_norm, moe_router, moe_router_bias, moe_w_gate_up, moe_w_down,
              shared_w_gate_up, shared_w_down, ln_gain, ln_bias):
    cos, sin = rotary_tables(x.shape[1])
    lb = jax.nn.softmax(hgrn_lower_bounds.astype(jnp.float32), axis=0)
    lb = jnp.cumsum(lb, axis=0) - lb[0]
    for layer in range(DEPTH):
        j = layer // N_MIXERS
        if layer % N_MIXERS == 0:
            lambda_init = 0.8 - 0.6 * math.exp(-0.3 * layer)
            h = diff_attention(x, attn_w_in[j], attn_w_out[j], attn_lambda[j], attn_subln[j], lambda_init, cos, sin)
        else:
            h = hgrn2(x, hgrn_w_in[j], hgrn_w_out[j], lb[layer], hgrn_norm[j])
        x = layer_norm(ALPHA * x + h, ln_gain[layer, 0], ln_bias[layer, 0])
        h = moe(x, moe_router[layer], moe_router_bias[layer], moe_w_gate_up[layer], moe_w_down[layer],
                shared_w_gate_up[layer], shared_w_down[layer])
        x = layer_norm(ALPHA * x + h, ln_gain[layer, 1], ln_bias[layer, 1])
    return x
```

```python
import functools
import math

import jax
import jax.numpy as jnp
from jax import lax
from jax.experimental import pallas as pl
from jax.experimental.pallas import tpu as pltpu

F32 = jnp.float32
BF16 = jnp.bfloat16

D_MODEL = 1024
DEPTH = 4
CHUNK = 64
A_HEADS = 8
A_HEAD_DIM = 64
ROT_DIM = 16
ROPE_THETA = 500000.0
H_HEADS = 8
H_EXPAND = 128
N_EXPERTS = 64
N_GROUPS = 8
TOPK_GROUP = 4
TOP_K = 8
EXPERT_FF = 256
ROUTED_SCALE = 2.5
ALPHA = (2 * DEPTH) ** 0.25
LN_EPS = 1e-5
LANES = 128
VMEM_LIMIT = 48 * 1024 * 1024


def _cparams(sem):
    return pltpu.CompilerParams(dimension_semantics=sem, vmem_limit_bytes=VMEM_LIMIT)


def _layer_norm(y, g, b):
    mu = jnp.mean(y, axis=-1, keepdims=True)
    yc = y - mu
    var = jnp.mean(yc * yc, axis=-1, keepdims=True)
    return yc * lax.rsqrt(var + LN_EPS) * g + b


def _dot_nt(a, b):
    return lax.dot_general(a, b, (((1,), (1,)), ((), ())), preferred_element_type=F32)


def _dot_tn(a, b):
    return lax.dot_general(a, b, (((0,), (0,)), ((), ())), preferred_element_type=F32)


def _attn_proj_kernel(x_ref, w_ref, cf_ref, s1_ref, s2_ref, q_ref, k_ref, v_ref):
    xb = x_ref[...].astype(BF16)
    cf, s1, s2 = cf_ref[...], s1_ref[...], s2_ref[...]

    def rotary(y):
        outs = []
        for c in range(D_MODEL // LANES):
            yc = y[:, c * LANES:(c + 1) * LANES]
            outs.append(yc * cf + pltpu.roll(yc, LANES - ROT_DIM // 2, 1) * s1
                        + pltpu.roll(yc, ROT_DIM // 2, 1) * s2)
        return jnp.concatenate(outs, axis=1)

    q = jnp.dot(xb, w_ref[:, 0:D_MODEL], preferred_element_type=F32)
    q_ref[...] = (rotary(q) * (A_HEAD_DIM ** -0.5)).astype(BF16)
    k = jnp.dot(xb, w_ref[:, D_MODEL:2 * D_MODEL], preferred_element_type=F32)
    k_ref[...] = rotary(k).astype(BF16)
    v = jnp.dot(xb, w_ref[:, 2 * D_MODEL:3 * D_MODEL], preferred_element_type=F32)
    v_ref[...] = v.astype(BF16)


def _attn_proj(x2, w, cf, s1, s2, seq, tm):
    t = x2.shape[0]
    nseq = seq // tm
    row = lambda i: (i, 0)
    tab = lambda i: (i % nseq, 0)
    out = jax.ShapeDtypeStruct((t, D_MODEL), BF16)
    return pl.pallas_call(
        _attn_proj_kernel,
        grid=(t // tm,),
        in_specs=[pl.BlockSpec((tm, D_MODEL), row),
                  pl.BlockSpec((D_MODEL, 3 * D_MODEL), lambda i: (0, 0)),
                  pl.BlockSpec((tm, LANES), tab), pl.BlockSpec((tm, LANES), tab),
                  pl.BlockSpec((tm, LANES), tab)],
        out_specs=[pl.BlockSpec((tm, D_MODEL), row)] * 3,
        out_shape=[out, out, out],
        compiler_params=_cparams(("arbitrary",)),
        name="attn_proj",
    )(x2, w, cf, s1, s2)


def _attn_kernel(lam_ref, q_ref, k_ref, v_ref, g_ref, o_ref,
                 m1_ref, l1_ref, a1_ref, m2_ref, l2_ref, a2_ref, *, tq, out_scale):
    i = pl.program_id(2)
    q = q_ref[0]
    hd = A_HEAD_DIM
    q1, q2 = q[:, :hd], q[:, hd:]

    for m_ref, l_ref, a_ref in ((m1_ref, l1_ref, a1_ref), (m2_ref, l2_ref, a2_ref)):
        m_ref[...] = jnp.full(m_ref.shape, -jnp.inf, F32)
        l_ref[...] = jnp.zeros(l_ref.shape, F32)
        a_ref[...] = jnp.zeros(a_ref.shape, F32)

    def update(s, vb, m_ref, l_ref, a_ref):
        m_old = m_ref[...]
        m_new = jnp.maximum(m_old, jnp.max(s, axis=-1, keepdims=True))
        alpha = jnp.exp(m_old - m_new)
        p = jnp.exp(s - m_new)
        l_ref[...] = alpha * l_ref[...] + jnp.sum(p, axis=-1, keepdims=True)
        a_ref[...] = alpha * a_ref[...] + jnp.dot(p.astype(BF16), vb, preferred_element_type=F32)
        m_ref[...] = m_new

    def block(j, mask):
        start = pl.multiple_of(j * tq, tq)
        kb = k_ref[0, pl.ds(start, tq), :]
        vb = v_ref[0, pl.ds(start, tq), :]
        s1 = _dot_nt(q1, kb[:, :hd])
        s2 = _dot_nt(q2, kb[:, hd:])
        if mask is not None:
            s1 = jnp.where(mask, s1, -jnp.inf)
            s2 = jnp.where(mask, s2, -jnp.inf)
        update(s1, vb, m1_ref, l1_ref, a1_ref)
        update(s2, vb, m2_ref, l2_ref, a2_ref)

    def full_block(j, carry):
        block(j, None)
        return carry

    lax.fori_loop(0, i, full_block, 0)
    rq = lax.broadcasted_iota(jnp.int32, (tq, tq), 0) // CHUNK
    ck = lax.broadcasted_iota(jnp.int32, (tq, tq), 1) // CHUNK
    block(i, ck <= rq)

    lam = lam_ref[0]
    o = a1_ref[...] / l1_ref[...] - lam * (a2_ref[...] / l2_ref[...])
    o = o * lax.rsqrt(jnp.mean(o * o, axis=-1, keepdims=True) + LN_EPS) * g_ref[...]
    o_ref[0] = (o * out_scale).astype(o_ref.dtype)


def _attention(lam, q, k, v, g, lambda_init, tq):
    b, s, _ = q.shape
    kern = functools.partial(_attn_kernel, tq=tq, out_scale=1.0 - lambda_init)
    kv_spec = pl.BlockSpec((1, s, 2 * A_HEAD_DIM), lambda bi, h, i: (bi, 0, h))
    qo_spec = pl.BlockSpec((1, tq, 2 * A_HEAD_DIM), lambda bi, h, i: (bi, i, h))
    vec = lambda n: pltpu.VMEM((tq, n), F32)
    return pl.pallas_call(
        kern,
        grid=(b, A_HEADS, s // tq),
        in_specs=[pl.BlockSpec(memory_space=pltpu.SMEM), qo_spec, kv_spec, kv_spec,
                  pl.BlockSpec((1, 2 * A_HEAD_DIM), lambda bi, h, i: (0, 0))],
        out_specs=qo_spec,
        out_shape=jax.ShapeDtypeStruct((b, s, D_MODEL), BF16),
        scratch_shapes=[vec(1), vec(1), vec(2 * A_HEAD_DIM), vec(1), vec(1), vec(2 * A_HEAD_DIM)],
        compiler_params=_cparams(("arbitrary", "arbitrary", "arbitrary")),
        name="diff_attention",
    )(lam, q, k, v, g)


def _out_proj_kernel(o_ref, x_ref, w_ref, g_ref, b_ref, y_ref):
    h = jnp.dot(o_ref[...], w_ref[...], preferred_element_type=F32)
    y_ref[...] = _layer_norm(ALPHA * x_ref[...] + h, g_ref[...], b_ref[...])


def _out_proj(o2, x2, w, g, b, tm):
    t = x2.shape[0]
    row = lambda i: (i, 0)
    fixed = lambda i: (0, 0)
    return pl.pallas_call(
        _out_proj_kernel,
        grid=(t // tm,),
        in_specs=[pl.BlockSpec((tm, D_MODEL), row), pl.BlockSpec((tm, D_MODEL), row),
                  pl.BlockSpec((D_MODEL, D_MODEL), fixed),
                  pl.BlockSpec((1, D_MODEL), fixed), pl.BlockSpec((1, D_MODEL), fixed)],
        out_specs=pl.BlockSpec((tm, D_MODEL), row),
        out_shape=jax.ShapeDtypeStruct((t, D_MODEL), F32),
        compiler_params=_cparams(("arbitrary",)),
        name="out_proj_ln",
    )(o2, x2, w, g, b)


def _rotary_tables(seq):
    pos = jnp.arange(seq, dtype=F32)
    inv_freq = ROPE_THETA ** (-jnp.arange(0, ROT_DIM, 2, dtype=F32) / ROT_DIM)
    ang = pos[:, None] * inv_freq[None, :]
    cos, sin = jnp.cos(ang), jnp.sin(ang)
    half = ROT_DIM // 2
    pad = A_HEAD_DIM - ROT_DIM
    one = jnp.ones((seq, pad), F32)
    zero = jnp.zeros((seq, pad), F32)
    zh = jnp.zeros((seq, half), F32)
    cf = jnp.concatenate([cos, cos, one], axis=1)
    s1 = jnp.concatenate([-sin, zh, zero], axis=1)
    s2 = jnp.concatenate([zh, sin, zero], axis=1)
    rep = LANES // A_HEAD_DIM
    return tuple(jnp.tile(t, (1, rep)) for t in (cf, s1, s2))


def _attn_layer(x2, bsz, seq, w_in, w_out, lam_params, subln, lambda_init, ln_g, ln_b, tables,
                tm=512, tq=512):
    q, k, v = _attn_proj(x2, w_in.astype(BF16), *tables, seq, tm)
    lp = lam_params.astype(F32)
    lam = jnp.exp(jnp.sum(lp[0] * lp[1])) - jnp.exp(jnp.sum(lp[2] * lp[3])) + lambda_init
    shp = (bsz, seq, D_MODEL)
    o = _attention(lam.reshape(1), q.reshape(shp), k.reshape(shp), v.reshape(shp),
                   subln.reshape(1, -1).astype(F32), lambda_init, tq)
    return _out_proj(o.reshape(bsz * seq, D_MODEL), x2, w_out.astype(BF16),
                     ln_g.reshape(1, -1), ln_b.reshape(1, -1), tm)


H_CHUNK = 128
H_SUB = 16


def _hgrn_proj_kernel(x_ref, w_ref, lb_ref, q_ref, k_ref, v_ref, lf_ref, g_ref):
    xb = x_ref[...].astype(BF16)
    w = H_HEADS * H_EXPAND
    q = jnp.dot(xb, w_ref[:, 0:w], preferred_element_type=F32)
    q_ref[...] = q * jax.nn.sigmoid(q)
    f = jnp.dot(xb, w_ref[:, w:2 * w], preferred_element_type=F32)
    lb = lb_ref[...]
    forget = lb + (1.0 - lb) * jax.nn.sigmoid(f)
    lf_ref[...] = jnp.log(forget)
    k_ref[...] = 1.0 - forget
    v_ref[...] = jnp.dot(xb, w_ref[:, 2 * w:3 * w], preferred_element_type=F32)
    g = jnp.dot(xb, w_ref[:, 3 * w:4 * w], preferred_element_type=F32)
    g_ref[...] = g * jax.nn.sigmoid(g)


def _hgrn_proj(x2, w, lb, tm):
    t = x2.shape[0]
    row = lambda i: (i, 0)
    fixed = lambda i: (0, 0)
    out = jax.ShapeDtypeStruct((t, D_MODEL), F32)
    return pl.pallas_call(
        _hgrn_proj_kernel,
        grid=(t // tm,),
        in_specs=[pl.BlockSpec((tm, D_MODEL), row), pl.BlockSpec((D_MODEL, 4 * D_MODEL), fixed),
                  pl.BlockSpec((1, D_MODEL), fixed)],
        out_specs=[pl.BlockSpec((tm, D_MODEL), row)] * 5,
        out_shape=[out] * 5,
        compiler_params=_cparams(("arbitrary",)),
        name="hgrn_proj",
    )(x2, w, lb)


def _hgrn_chunk(q, k, v, lf, st):
    c, nb = H_CHUNK, H_CHUNK // H_SUB
    r_i = lax.broadcasted_iota(jnp.int32, (c, c), 0)
    c_i = lax.broadcasted_iota(jnp.int32, (c, c), 1)
    tri = (r_i >= c_i).astype(BF16)
    hi = lf.astype(BF16)
    lo = (lf - hi.astype(F32)).astype(BF16)
    b2 = jnp.dot(tri, jnp.concatenate([hi, lo], axis=1), preferred_element_type=F32)
    b = b2[:, :H_EXPAND] + b2[:, H_EXPAND:]
    b_last = b[c - 1:c, :]

    o = _dot_nt((q * jnp.exp(b)).astype(BF16), st.astype(BF16))
    ke = k * jnp.exp(b_last - b)
    st_new = st * jnp.exp(b_last) + _dot_tn(v.astype(BF16), ke.astype(BF16))

    ends = [b[(j + 1) * H_SUB - 1:(j + 1) * H_SUB, :] for j in range(nb)]
    e_blk = jnp.concatenate([jnp.broadcast_to(e, (H_SUB, H_EXPAND)) for e in ends], axis=0)
    kt = k * jnp.exp(e_blk - b)
    row_blk = lax.broadcasted_iota(jnp.int32, (c, H_EXPAND), 0) // H_SUB
    q_big = jnp.concatenate(
        [(q * jnp.exp(jnp.minimum(b - ends[j], 0.0))).astype(BF16) for j in range(nb)], axis=1)
    k_big = jnp.concatenate(
        [jnp.where(row_blk == j, kt, 0.0).astype(BF16) for j in range(nb)], axis=1)
    a = _dot_nt(q_big, k_big)
    a = jnp.where(r_i // H_SUB > c_i // H_SUB, a, 0.0)
    o = o + jnp.dot(a.astype(BF16), v.astype(BF16), preferred_element_type=F32)

    t_idx = lax.broadcasted_iota(jnp.int32, (H_SUB, 1), 0)
    diag = []
    for j in range(nb):
        sl = slice(j * H_SUB, (j + 1) * H_SUB)
        qb, kb, vb, bb = q[sl], k[sl], v[sl], b[sl]
        od = jnp.zeros((H_SUB, H_EXPAND), F32)
        for s in range(H_SUB):
            e = jnp.exp(jnp.minimum(bb - bb[s:s + 1, :], 0.0))
            col = jnp.sum(qb * (kb[s:s + 1, :] * e), axis=-1, keepdims=True)
            od = od + jnp.where(t_idx >= s, col, 0.0) * vb[s:s + 1, :]
        diag.append(od)
    return o + jnp.concatenate(diag, axis=0), st_new


def _hgrn_kernel(q_ref, k_ref, v_ref, lf_ref, g_ref, ng_ref, o_ref, st_ref, *, n_chunks):
    @pl.when(pl.program_id(2) == 0)
    def _():
        st_ref[...] = jnp.zeros(st_ref.shape, F32)

    def body(ci, carry):
        sl = pl.ds(pl.multiple_of(ci * H_CHUNK, H_CHUNK), H_CHUNK)
        o, st_new = _hgrn_chunk(q_ref[sl, :], k_ref[sl, :], v_ref[sl, :], lf_ref[sl, :], st_ref[...])
        st_ref[...] = st_new
        o = o * lax.rsqrt(jnp.mean(o * o, axis=-1, keepdims=True) + LN_EPS) * ng_ref[...]
        o_ref[sl, :] = (o * g_ref[sl, :]).astype(o_ref.dtype)
        return carry

    lax.fori_loop(0, n_chunks, body, 0)


def _hgrn_recurrence(q, k, v, lf, g, ng, bsz, seq, tc):
    t = q.shape[0]
    nt = seq // tc
    blk = pl.BlockSpec((tc, H_EXPAND), lambda b, h, c: (b * nt + c, h))
    return pl.pallas_call(
        functools.partial(_hgrn_kernel, n_chunks=tc // H_CHUNK),
        grid=(bsz, H_HEADS, nt),
        in_specs=[blk] * 5 + [pl.BlockSpec((1, H_EXPAND), lambda b, h, c: (0, 0))],
        out_specs=blk,
        out_shape=jax.ShapeDtypeStruct((t, D_MODEL), BF16),
        scratch_shapes=[pltpu.VMEM((H_EXPAND, H_EXPAND), F32)],
        compiler_params=_cparams(("arbitrary", "arbitrary", "arbitrary")),
        name="hgrn_recurrence",
    )(q, k, v, lf, g, ng)


def _hgrn_layer(x2, bsz, seq, w_in, w_out, lb, norm_g, ln_g, ln_b, tm=256, tc=512):
    q, k, v, lf, g = _hgrn_proj(x2, w_in.astype(BF16), lb.reshape(1, -1).astype(F32), tm)
    o = _hgrn_recurrence(q, k, v, lf, g, norm_g.reshape(1, -1).astype(F32), bsz, seq, tc)
    return _out_proj(o, x2, w_out.astype(BF16), ln_g.reshape(1, -1), ln_b.reshape(1, -1), 2 * tm)


GROUP_SIZE = N_EXPERTS // N_GROUPS
X_WORDS = D_MODEL // 2
X_SLAB = X_WORDS // LANES
Y_SLAB = D_MODEL // LANES


def _router_kernel(x_ref, wt_ref, bias_ref, idx_ref, rank_ref, gate_tm_ref, cnt_ref,
                   carry_ref, *, tr):
    @pl.when(pl.program_id(0) == 0)
    def _():
        carry_ref[...] = jnp.zeros(carry_ref.shape, F32)

    e_n = N_EXPERTS
    logits = lax.dot_general(wt_ref[...], x_ref[...], (((1,), (1,)), ((), ())),
                             precision=lax.Precision.HIGHEST, preferred_element_type=F32)
    scores = jax.nn.sigmoid(logits)
    choice = scores + bias_ref[...]

    ch3 = choice.reshape(N_GROUPS, GROUP_SIZE, tr)
    sub = lax.broadcasted_iota(jnp.int32, ch3.shape, 1)
    m1 = jnp.max(ch3, axis=1, keepdims=True)
    first = jnp.min(jnp.where(ch3 == m1, sub, GROUP_SIZE), axis=1, keepdims=True)
    m2 = jnp.max(jnp.where(sub == first, -jnp.inf, ch3), axis=1, keepdims=True)
    gs = (m1 + m2).reshape(N_GROUPS, tr)

    g_i = lax.broadcasted_iota(jnp.int32, gs.shape, 0)
    g_rank = jnp.zeros(gs.shape, F32)
    for g in range(N_GROUPS):
        row = gs[g:g + 1, :]
        ahead = (row > gs) | ((row == gs) & (g_i > g))
        g_rank = g_rank + jnp.where(ahead, 1.0, 0.0)
    g_keep = (g_rank < TOPK_GROUP).astype(F32).reshape(N_GROUPS, 1, tr)
    keep = jnp.broadcast_to(g_keep, (N_GROUPS, GROUP_SIZE, tr)).reshape(e_n, tr) > 0.5
    cm = jnp.where(keep, choice, -jnp.inf)

    e_i = lax.broadcasted_iota(jnp.int32, cm.shape, 0)
    e_rank = jnp.zeros(cm.shape, F32)
    for e in range(e_n):
        row = cm[e:e + 1, :]
        ahead = (row > cm) | ((row == cm) & (e_i > e))
        e_rank = e_rank + jnp.where(ahead, 1.0, 0.0)
    sel = e_rank < TOP_K
    sel_f = sel.astype(F32)
    w = jnp.where(sel, scores, 0.0)
    gate = w / (jnp.sum(w, axis=0, keepdims=True) + 1e-20) * ROUTED_SCALE

    sel_b = sel_f.astype(BF16)
    t_r = lax.broadcasted_iota(jnp.int32, (tr, tr), 0)
    t_c = lax.broadcasted_iota(jnp.int32, (tr, tr), 1)
    before = jnp.dot(sel_b, (t_r < t_c).astype(BF16), preferred_element_type=F32)
    tok_rank = carry_ref[:, 0:1] + before
    carry_ref[...] = carry_ref[...] + jnp.sum(sel_f, axis=1, keepdims=True)
    cnt_ref[...] = carry_ref[...]

    x_r = lax.broadcasted_iota(jnp.int32, (e_n, e_n), 0)
    x_c = lax.broadcasted_iota(jnp.int32, (e_n, e_n), 1)
    slot = jnp.dot((x_c < x_r).astype(BF16), sel_b, preferred_element_type=F32)
    e_f = e_i.astype(F32)
    idx_rows, gate_rows, rank_rows = [], [], []
    for j in range(TOP_K):
        pick = jnp.where(sel & (slot == j), 1.0, 0.0)
        idx_rows.append(jnp.sum(pick * e_f, axis=0, keepdims=True))
        gate_rows.append(jnp.sum(pick * gate, axis=0, keepdims=True))
        rank_rows.append(jnp.sum(pick * tok_rank, axis=0, keepdims=True))
    idx_ref[...] = jnp.concatenate(idx_rows, axis=0).astype(jnp.int32)
    rank_ref[...] = jnp.concatenate(rank_rows, axis=0).astype(jnp.int32)
    gates = jnp.concatenate(gate_rows, axis=0)
    padded = jnp.concatenate([gates, jnp.zeros((LANES - TOP_K, tr), F32)], axis=0)
    gate_tm_ref[...] = padded.T


def _router(x2, wt, bias, tr):
    t = x2.shape[0]
    col = lambda i: (0, i)
    return pl.pallas_call(
        functools.partial(_router_kernel, tr=tr),
        grid=(t // tr,),
        in_specs=[pl.BlockSpec((tr, D_MODEL), lambda i: (i, 0)),
                  pl.BlockSpec((N_EXPERTS, D_MODEL), lambda i: (0, 0)),
                  pl.BlockSpec((N_EXPERTS, 1), lambda i: (0, 0))],
        out_specs=[pl.BlockSpec((TOP_K, tr), col),
                   pl.BlockSpec((TOP_K, tr), col), pl.BlockSpec((tr, LANES), lambda i: (i, 0)),
                   pl.BlockSpec((N_EXPERTS, LANES), lambda i: (0, 0))],
        out_shape=[jax.ShapeDtypeStruct((TOP_K, t), jnp.int32),
                   jax.ShapeDtypeStruct((TOP_K, t), jnp.int32),
                   jax.ShapeDtypeStruct((t, LANES), F32),
                   jax.ShapeDtypeStruct((N_EXPERTS, LANES), F32)],
        scratch_shapes=[pltpu.VMEM((N_EXPERTS, LANES), F32)],
        compiler_params=_cparams(("arbitrary",)),
        name="moe_router",
    )(x2, wt, bias)


def _dispatch_kernel(dest_ref, x_ref, zeros_ref, xs_ref, xp_ref, sem, *, td):
    del zeros_ref
    x = x_ref[...]
    lo = pltpu.bitcast(x[:, :X_WORDS].astype(BF16).astype(F32), jnp.uint32)
    hi = pltpu.bitcast(x[:, X_WORDS:].astype(BF16).astype(F32), jnp.uint32)
    words = (lo >> 16) | (hi & jnp.uint32(0xFFFF0000))
    for j in range(X_SLAB):
        xp_ref[:, j, :] = words[:, j * LANES:(j + 1) * LANES]

    def issue(t, carry):
        for j in range(TOP_K):
            pltpu.make_async_copy(xp_ref.at[t], xs_ref.at[dest_ref[j, t]], sem).start()
        return carry

    lax.fori_loop(0, td, issue, 0)
    for j in range(TOP_K):
        pltpu.make_async_copy(xp_ref, xs_ref.at[pl.ds(0, td)], sem).wait()


def _dispatch(dest, x2, zeros, td):
    t = x2.shape[0]
    return pl.pallas_call(
        functools.partial(_dispatch_kernel, td=td),
        grid=(t // td,),
        in_specs=[pl.BlockSpec((TOP_K, td), lambda i: (0, i), memory_space=pltpu.SMEM),
                  pl.BlockSpec((td, D_MODEL), lambda i: (i, 0)),
                  pl.BlockSpec(memory_space=pl.ANY)],
        out_specs=pl.BlockSpec(memory_space=pl.ANY),
        out_shape=jax.ShapeDtypeStruct(zeros.shape, zeros.dtype),
        input_output_aliases={2: 0},
        scratch_shapes=[pltpu.VMEM((td, X_SLAB, LANES), jnp.uint32), pltpu.SemaphoreType.DMA],
        compiler_params=_cparams(("arbitrary",)),
        name="moe_dispatch",
    )(dest, x2, zeros)


def _unpack_rows(words):
    lo = pltpu.bitcast(words << 16, F32)
    hi = pltpu.bitcast(words & jnp.uint32(0xFFFF0000), F32)
    return jnp.concatenate([lo, hi], axis=1).astype(BF16)


def _swiglu(xb, w_gu, w_d):
    h = jnp.dot(xb, w_gu, preferred_element_type=F32)
    act = h[:, :EXPERT_FF] * jax.nn.sigmoid(h[:, :EXPERT_FF]) * h[:, EXPERT_FF:]
    return jnp.dot(act.astype(BF16), w_d, preferred_element_type=F32)


def _expert_kernel(blk_ref, exp_ref, used_ref, xs_ref, wgu_ref, wd_ref, ys_ref, *, te):
    del blk_ref, exp_ref
    live = pl.program_id(0) < used_ref[0]

    @pl.when(live)
    def _():
        words = jnp.concatenate(
            [xs_ref[pl.ds(j, te, stride=X_SLAB), :] for j in range(X_SLAB)], axis=1)
        y = _swiglu(_unpack_rows(words), wgu_ref[0], wd_ref[0])
        for c in range(Y_SLAB):
            ys_ref[pl.ds(c, te, stride=Y_SLAB), :] = y[:, c * LANES:(c + 1) * LANES]

    @pl.when(jnp.logical_not(live))
    def _():
        ys_ref[...] = jnp.zeros(ys_ref.shape, ys_ref.dtype)


def _experts(blk, exp, used, xs2, w_gu, w_d, te):
    n_blocks = blk.shape[0]
    n_rows = xs2.shape[0] // X_SLAB
    return pl.pallas_call(
        functools.partial(_expert_kernel, te=te),
        grid_spec=pltpu.PrefetchScalarGridSpec(
            num_scalar_prefetch=3,
            grid=(n_blocks,),
            in_specs=[pl.BlockSpec((te * X_SLAB, LANES), lambda b, blk, exp, used: (blk[b], 0)),
                      pl.BlockSpec((1, D_MODEL, 2 * EXPERT_FF), lambda b, blk, exp, used: (exp[b], 0, 0)),
                      pl.BlockSpec((1, EXPERT_FF, D_MODEL), lambda b, blk, exp, used: (exp[b], 0, 0))],
            out_specs=pl.BlockSpec((te * Y_SLAB, LANES), lambda b, blk, exp, used: (b, 0)),
        ),
        out_shape=jax.ShapeDtypeStruct((n_rows * Y_SLAB, LANES), F32),
        compiler_params=_cparams(("arbitrary",)),
        name="moe_experts",
    )(blk, exp, used, xs2, w_gu, w_d)


def _combine_kernel(dest_ref, dest_next_ref, gate_ref, x_ref, wgu_ref, wd_ref, g_ref, b_ref, ys_ref,
                    o_ref, ybuf_ref, sem, *, tc):
    i = pl.program_id(0)
    n = pl.num_programs(0)
    slot = i % 2
    rows = TOP_K * tc

    def row_copy(src_row, dst_row, into):
        src = ys_ref.at[pl.ds(pl.multiple_of(src_row * Y_SLAB, Y_SLAB), Y_SLAB)]
        dst = ybuf_ref.at[pl.ds(pl.multiple_of(dst_row * Y_SLAB, Y_SLAB), Y_SLAB)]
        return pltpu.make_async_copy(src, dst, sem.at[into])

    def gather(d_ref, into):
        def issue(t, carry):
            for j in range(TOP_K):
                row_copy(d_ref[j, t], into * rows + j * tc + t, into).start()
            return carry
        lax.fori_loop(0, tc, issue, 0)

    @pl.when(i == 0)
    def _():
        gather(dest_ref, 0)

    @pl.when(i + 1 < n)
    def _():
        gather(dest_next_ref, 1 - slot)

    x = x_ref[...]
    acc = ALPHA * x + _swiglu(x.astype(BF16), wgu_ref[...], wd_ref[...])

    base = pl.multiple_of(slot * rows * Y_SLAB, rows * Y_SLAB)
    pltpu.make_async_copy(ys_ref.at[pl.ds(0, rows * Y_SLAB)],
                          ybuf_ref.at[pl.ds(base, rows * Y_SLAB)], sem.at[slot]).wait()
    gate = gate_ref[...]
    for j in range(TOP_K):
        y_j = jnp.concatenate(
            [ybuf_ref[pl.ds(base + j * tc * Y_SLAB + c, tc, stride=Y_SLAB), :] for c in range(Y_SLAB)],
            axis=1)
        acc = acc + gate[:, j:j + 1] * y_j
    o_ref[...] = _layer_norm(acc, g_ref[...], b_ref[...])


def _combine(dest, gate_tm, x2, ys, ws_gu, ws_d, g, b, tc):
    t = x2.shape[0]
    n = t // tc
    fixed = lambda i: (0, 0)
    row = lambda i: (i, 0)
    smem = lambda f: pl.BlockSpec((TOP_K, tc), f, memory_space=pltpu.SMEM)
    return pl.pallas_call(
        functools.partial(_combine_kernel, tc=tc),
        grid=(n,),
        in_specs=[smem(lambda i: (0, i)), smem(lambda i: (0, jnp.minimum(i + 1, n - 1))),
                  pl.BlockSpec((tc, LANES), row), pl.BlockSpec((tc, D_MODEL), row),
                  pl.BlockSpec((D_MODEL, 2 * EXPERT_FF), fixed), pl.BlockSpec((EXPERT_FF, D_MODEL), fixed),
                  pl.BlockSpec((1, D_MODEL), fixed), pl.BlockSpec((1, D_MODEL), fixed),
                  pl.BlockSpec(memory_space=pl.ANY)],
        out_specs=pl.BlockSpec((tc, D_MODEL), row),
        out_shape=jax.ShapeDtypeStruct((t, D_MODEL), F32),
        scratch_shapes=[pltpu.VMEM((2 * TOP_K * tc * Y_SLAB, LANES), F32),
                        pltpu.SemaphoreType.DMA((2,))],
        compiler_params=_cparams(("arbitrary",)),
        name="moe_combine",
    )(dest, dest, gate_tm, x2, ws_gu, ws_d, g, b, ys)


def _moe_layer(x2, w_router, bias, w_gu, w_d, ws_gu, ws_d, ln_g, ln_b, te=256, tr=512, td=256, tc=128):
    t = x2.shape[0]
    n_blocks = t * TOP_K // te + N_EXPERTS
    n_rows = n_blocks * te
    idx, rank, gate_tm, cnt = _router(x2, w_router.T.astype(F32), bias.reshape(-1, 1).astype(F32), tr)

    counts = cnt[:, 0].astype(jnp.int32)
    padded = (counts + te - 1) // te * te
    pad_end = jnp.cumsum(padded)
    pad_start = pad_end - padded
    dest = pad_start[idx] + rank
    used = pad_end[-1] // te
    blk = jnp.minimum(jnp.arange(n_blocks, dtype=jnp.int32), used - 1)
    exp = jnp.minimum(jnp.searchsorted(pad_end // te, blk, side="right"), N_EXPERTS - 1)

    xs = _dispatch(dest, x2, jnp.zeros((n_rows, X_SLAB, LANES), jnp.uint32), td)
    ys = _experts(blk, exp.astype(jnp.int32), used.reshape(1), xs.reshape(n_rows * X_SLAB, LANES),
                  w_gu.astype(BF16), w_d.astype(BF16), te)
    return _combine(dest, gate_tm, x2, ys, ws_gu.astype(BF16), ws_d.astype(BF16),
                    ln_g.reshape(1, -1), ln_b.reshape(1, -1), tc)


def kernel(x, attn_w_in, attn_w_out, attn_lambda, attn_subln, hgrn_w_in, hgrn_w_out, hgrn_lower_bounds, hgrn_norm, moe_router, moe_router_bias, moe_w_gate_up, moe_w_down, shared_w_gate_up, shared_w_down, ln_gain, ln_bias):
    bsz, seq, _ = x.shape
    tables = _rotary_tables(seq)
    lb = jax.nn.softmax(hgrn_lower_bounds.astype(F32), axis=0)
    lb = jnp.cumsum(lb, axis=0) - lb[0]
    x2 = x.reshape(bsz * seq, D_MODEL)
    for layer in range(DEPTH):
        j = layer // 2
        if layer % 2 == 0:
            lambda_init = 0.8 - 0.6 * math.exp(-0.3 * layer)
            x2 = _attn_layer(x2, bsz, seq, attn_w_in[j], attn_w_out[j], attn_lambda[j], attn_subln[j],
                             lambda_init, ln_gain[layer, 0], ln_bias[layer, 0], tables)
        else:
            x2 = _hgrn_layer(x2, bsz, seq, hgrn_w_in[j], hgrn_w_out[j], lb[layer], hgrn_norm[j],
                             ln_gain[layer, 0], ln_bias[layer, 0])
        x2 = _moe_layer(x2, moe_router[layer], moe_router_bias[layer], moe_w_gate_up[layer],
                        moe_w_down[layer], shared_w_gate_up[layer], shared_w_down[layer],
                        ln_gain[layer, 1], ln_bias[layer, 1])
    return x2.reshape(bsz, seq, D_MODEL)
```

```python
import functools
import math

import jax
import jax.numpy as jnp
from jax import lax
from jax.experimental import pallas as pl
from jax.experimental.pallas import tpu as pltpu

F32 = jnp.float32
BF16 = jnp.bfloat16

D_MODEL = 1024
DEPTH = 4
CHUNK = 64
A_HEADS = 8
A_HEAD_DIM = 64
ROT_DIM = 16
ROPE_THETA = 500000.0
H_HEADS = 8
H_EXPAND = 128
N_EXPERTS = 64
N_GROUPS = 8
TOPK_GROUP = 4
TOP_K = 8
EXPERT_FF = 256
ROUTED_SCALE = 2.5
ALPHA = (2 * DEPTH) ** 0.25
LN_EPS = 1e-5
LANES = 128
VMEM_LIMIT = 48 * 1024 * 1024


def _cparams(sem):
    return pltpu.CompilerParams(dimension_semantics=sem, vmem_limit_bytes=VMEM_LIMIT)


def _layer_norm(y, g, b):
    mu = jnp.mean(y, axis=-1, keepdims=True)
    yc = y - mu
    var = jnp.mean(yc * yc, axis=-1, keepdims=True)
    return yc * lax.rsqrt(var + LN_EPS) * g + b


def _dot_nt(a, b):
    return lax.dot_general(a, b, (((1,), (1,)), ((), ())), preferred_element_type=F32)


def _dot_tn(a, b):
    return lax.dot_general(a, b, (((0,), (0,)), ((), ())), preferred_element_type=F32)


def _attn_proj_kernel(x_ref, w_ref, cf_ref, s1_ref, s2_ref, q_ref, k_ref, v_ref):
    xb = x_ref[...].astype(BF16)
    cf, s1, s2 = cf_ref[...], s1_ref[...], s2_ref[...]

    def rotary(y):
        outs = []
        for c in range(D_MODEL // LANES):
            yc = y[:, c * LANES:(c + 1) * LANES]
            outs.append(yc * cf + pltpu.roll(yc, LANES - ROT_DIM // 2, 1) * s1
                        + pltpu.roll(yc, ROT_DIM // 2, 1) * s2)
        return jnp.concatenate(outs, axis=1)

    q = jnp.dot(xb, w_ref[:, 0:D_MODEL], preferred_element_type=F32)
    q_ref[...] = (rotary(q) * (A_HEAD_DIM ** -0.5 * math.log2(math.e))).astype(BF16)
    k = jnp.dot(xb, w_ref[:, D_MODEL:2 * D_MODEL], preferred_element_type=F32)
    k_ref[...] = rotary(k).astype(BF16)
    v = jnp.dot(xb, w_ref[:, 2 * D_MODEL:3 * D_MODEL], preferred_element_type=F32)
    v_ref[...] = v.astype(BF16)


def _attn_proj(x2, w, cf, s1, s2, seq, tm):
    t = x2.shape[0]
    nseq = seq // tm
    row = lambda i: (i, 0)
    tab = lambda i: (i % nseq, 0)
    out = jax.ShapeDtypeStruct((t, D_MODEL), BF16)
    return pl.pallas_call(
        _attn_proj_kernel,
        grid=(t // tm,),
        in_specs=[pl.BlockSpec((tm, D_MODEL), row),
                  pl.BlockSpec((D_MODEL, 3 * D_MODEL), lambda i: (0, 0)),
                  pl.BlockSpec((tm, LANES), tab), pl.BlockSpec((tm, LANES), tab),
                  pl.BlockSpec((tm, LANES), tab)],
        out_specs=[pl.BlockSpec((tm, D_MODEL), row)] * 3,
        out_shape=[out, out, out],
        compiler_params=_cparams(("arbitrary",)),
        name="attn_proj",
    )(x2, w, cf, s1, s2)


def _attn_kernel(lam_ref, q_ref, k_ref, v_ref, g_ref, o_ref,
                 m1_ref, l1_ref, a1_ref, m2_ref, l2_ref, a2_ref, *, tq, out_scale):
    i = pl.program_id(2)
    q = q_ref[0]
    lane = lax.broadcasted_iota(jnp.int32, q.shape, 1)
    zero = jnp.zeros_like(q)
    q1 = jnp.where(lane < A_HEAD_DIM, q, zero)
    q2 = jnp.where(lane >= A_HEAD_DIM, q, zero)

    for m_ref, l_ref, a_ref in ((m1_ref, l1_ref, a1_ref), (m2_ref, l2_ref, a2_ref)):
        m_ref[...] = jnp.full(m_ref.shape, -jnp.inf, F32)
        l_ref[...] = jnp.zeros(l_ref.shape, F32)
        a_ref[...] = jnp.zeros(a_ref.shape, F32)

    def update(s, vb, m_ref, l_ref, a_ref):
        tiles = [s[:, c * LANES:(c + 1) * LANES] for c in range(s.shape[1] // LANES)]
        mc = functools.reduce(jnp.maximum, tiles)
        m_old = m_ref[...]
        m_new = jnp.maximum(m_old, jnp.max(mc, axis=-1, keepdims=True))
        alpha = jnp.exp2(m_old - m_new)
        p = [jnp.exp2(t - m_new) for t in tiles]
        l_ref[...] = alpha * l_ref[...] + functools.reduce(jnp.add, p)
        pb = jnp.concatenate([t.astype(BF16) for t in p], axis=1)
        a_ref[...] = alpha * a_ref[...] + jnp.dot(pb, vb, preferred_element_type=F32)
        m_ref[...] = m_new

    def block(j, mask):
        start = pl.multiple_of(j * tq, tq)
        kb = k_ref[0, pl.ds(start, tq), :]
        vb = v_ref[0, pl.ds(start, tq), :]
        s1 = _dot_nt(q1, kb)
        s2 = _dot_nt(q2, kb)
        if mask is not None:
            s1 = jnp.where(mask, s1, -jnp.inf)
            s2 = jnp.where(mask, s2, -jnp.inf)
        update(s1, vb, m1_ref, l1_ref, a1_ref)
        update(s2, vb, m2_ref, l2_ref, a2_ref)

    def full_block(j, carry):
        block(j, None)
        return carry

    lax.fori_loop(0, i, full_block, 0)
    rq = lax.broadcasted_iota(jnp.int32, (tq, tq), 0) // CHUNK
    ck = lax.broadcasted_iota(jnp.int32, (tq, tq), 1) // CHUNK
    block(i, ck <= rq)

    lam = lam_ref[0]
    l1 = jnp.sum(l1_ref[...], axis=-1, keepdims=True)
    l2 = jnp.sum(l2_ref[...], axis=-1, keepdims=True)
    o = a1_ref[...] / l1 - lam * (a2_ref[...] / l2)
    o = o * lax.rsqrt(jnp.mean(o * o, axis=-1, keepdims=True) + LN_EPS) * g_ref[...]
    o_ref[0] = (o * out_scale).astype(o_ref.dtype)


def _attention(lam, q, k, v, g, lambda_init, tq):
    b, s, _ = q.shape
    kern = functools.partial(_attn_kernel, tq=tq, out_scale=1.0 - lambda_init)
    kv_spec = pl.BlockSpec((1, s, 2 * A_HEAD_DIM), lambda bi, h, i: (bi, 0, h))
    qo_spec = pl.BlockSpec((1, tq, 2 * A_HEAD_DIM), lambda bi, h, i: (bi, i, h))
    vec = lambda n: pltpu.VMEM((tq, n), F32)
    return pl.pallas_call(
        kern,
        grid=(b, A_HEADS, s // tq),
        in_specs=[pl.BlockSpec(memory_space=pltpu.SMEM), qo_spec, kv_spec, kv_spec,
                  pl.BlockSpec((1, 2 * A_HEAD_DIM), lambda bi, h, i: (0, 0))],
        out_specs=qo_spec,
        out_shape=jax.ShapeDtypeStruct((b, s, D_MODEL), BF16),
        scratch_shapes=[vec(LANES)] * 6,
        compiler_params=_cparams(("arbitrary", "arbitrary", "arbitrary")),
        name="diff_attention",
    )(lam, q, k, v, g)


def _out_proj_kernel(o_ref, x_ref, w_ref, g_ref, b_ref, y_ref):
    h = jnp.dot(o_ref[...], w_ref[...], preferred_element_type=F32)
    y_ref[...] = _layer_norm(ALPHA * x_ref[...] + h, g_ref[...], b_ref[...])


def _out_proj(o2, x2, w, g, b, tm):
    t = x2.shape[0]
    row = lambda i: (i, 0)
    fixed = lambda i: (0, 0)
    return pl.pallas_call(
        _out_proj_kernel,
        grid=(t // tm,),
        in_specs=[pl.BlockSpec((tm, D_MODEL), row), pl.BlockSpec((tm, D_MODEL), row),
                  pl.BlockSpec((D_MODEL, D_MODEL), fixed),
                  pl.BlockSpec((1, D_MODEL), fixed), pl.BlockSpec((1, D_MODEL), fixed)],
        out_specs=pl.BlockSpec((tm, D_MODEL), row),
        out_shape=jax.ShapeDtypeStruct((t, D_MODEL), F32),
        compiler_params=_cparams(("arbitrary",)),
        name="out_proj_ln",
    )(o2, x2, w, g, b)


def _rotary_tables(seq):
    pos = jnp.arange(seq, dtype=F32)
    inv_freq = ROPE_THETA ** (-jnp.arange(0, ROT_DIM, 2, dtype=F32) / ROT_DIM)
    ang = pos[:, None] * inv_freq[None, :]
    cos, sin = jnp.cos(ang), jnp.sin(ang)
    half = ROT_DIM // 2
    pad = A_HEAD_DIM - ROT_DIM
    one = jnp.ones((seq, pad), F32)
    zero = jnp.zeros((seq, pad), F32)
    zh = jnp.zeros((seq, half), F32)
    cf = jnp.concatenate([cos, cos, one], axis=1)
    s1 = jnp.concatenate([-sin, zh, zero], axis=1)
    s2 = jnp.concatenate([zh, sin, zero], axis=1)
    rep = LANES // A_HEAD_DIM
    return tuple(jnp.tile(t, (1, rep)) for t in (cf, s1, s2))


def _attn_layer(x2, bsz, seq, w_in, w_out, lam_params, subln, lambda_init, ln_g, ln_b, tables,
                tm=512, tq=512):
    q, k, v = _attn_proj(x2, w_in.astype(BF16), *tables, seq, tm)
    lp = lam_params.astype(F32)
    lam = jnp.exp(jnp.sum(lp[0] * lp[1])) - jnp.exp(jnp.sum(lp[2] * lp[3])) + lambda_init
    shp = (bsz, seq, D_MODEL)
    o = _attention(lam.reshape(1), q.reshape(shp), k.reshape(shp), v.reshape(shp),
                   subln.reshape(1, -1).astype(F32), lambda_init, tq)
    return _out_proj(o.reshape(bsz * seq, D_MODEL), x2, w_out.astype(BF16),
                     ln_g.reshape(1, -1), ln_b.reshape(1, -1), tm)


H_CHUNK = 128
H_SUB = 16


def _hgrn_proj_kernel(x_ref, w_ref, lb_ref, q_ref, k_ref, v_ref, lf_ref, g_ref):
    xb = x_ref[...].astype(BF16)
    w = H_HEADS * H_EXPAND
    q = jnp.dot(xb, w_ref[:, 0:w], preferred_element_type=F32)
    q_ref[...] = q * jax.nn.sigmoid(q)
    f = jnp.dot(xb, w_ref[:, w:2 * w], preferred_element_type=F32)
    lb = lb_ref[...]
    forget = lb + (1.0 - lb) * jax.nn.sigmoid(f)
    lf_ref[...] = jnp.log(forget)
    k_ref[...] = 1.0 - forget
    v_ref[...] = jnp.dot(xb, w_ref[:, 2 * w:3 * w], preferred_element_type=F32)
    g = jnp.dot(xb, w_ref[:, 3 * w:4 * w], preferred_element_type=F32)
    g_ref[...] = g * jax.nn.sigmoid(g)


def _hgrn_proj(x2, w, lb, tm):
    t = x2.shape[0]
    row = lambda i: (i, 0)
    fixed = lambda i: (0, 0)
    out = jax.ShapeDtypeStruct((t, D_MODEL), F32)
    return pl.pallas_call(
        _hgrn_proj_kernel,
        grid=(t // tm,),
        in_specs=[pl.BlockSpec((tm, D_MODEL), row), pl.BlockSpec((D_MODEL, 4 * D_MODEL), fixed),
                  pl.BlockSpec((1, D_MODEL), fixed)],
        out_specs=[pl.BlockSpec((tm, D_MODEL), row)] * 5,
        out_shape=[out] * 5,
        compiler_params=_cparams(("arbitrary",)),
        name="hgrn_proj",
    )(x2, w, lb)


def _hgrn_chunk(q, k, v, lf, st):
    c, nb = H_CHUNK, H_CHUNK // H_SUB
    r_i = lax.broadcasted_iota(jnp.int32, (c, c), 0)
    c_i = lax.broadcasted_iota(jnp.int32, (c, c), 1)
    tri = (r_i >= c_i).astype(BF16)
    hi = lf.astype(BF16)
    lo = (lf - hi.astype(F32)).astype(BF16)
    b2 = jnp.dot(tri, jnp.concatenate([hi, lo], axis=1), preferred_element_type=F32)
    b = b2[:, :H_EXPAND] + b2[:, H_EXPAND:]
    b_last = b[c - 1:c, :]

    o = _dot_nt((q * jnp.exp(b)).astype(BF16), st.astype(BF16))
    ke = k * jnp.exp(b_last - b)
    st_new = st * jnp.exp(b_last) + _dot_tn(v.astype(BF16), ke.astype(BF16))

    ends = [b[(j + 1) * H_SUB - 1:(j + 1) * H_SUB, :] for j in range(nb)]
    e_blk = jnp.concatenate([jnp.broadcast_to(e, (H_SUB, H_EXPAND)) for e in ends], axis=0)
    kt = k * jnp.exp(e_blk - b)
    row_blk = lax.broadcasted_iota(jnp.int32, (c, H_EXPAND), 0) // H_SUB
    q_big = jnp.concatenate(
        [(q * jnp.exp(jnp.minimum(b - ends[j], 0.0))).astype(BF16) for j in range(nb)], axis=1)
    k_big = jnp.concatenate(
        [jnp.where(row_blk == j, kt, 0.0).astype(BF16) for j in range(nb)], axis=1)
    a = _dot_nt(q_big, k_big)
    a = jnp.where(r_i // H_SUB > c_i // H_SUB, a, 0.0)
    o = o + jnp.dot(a.astype(BF16), v.astype(BF16), preferred_element_type=F32)

    t_idx = lax.broadcasted_iota(jnp.int32, (H_SUB, 1), 0)
    diag = []
    for j in range(nb):
        sl = slice(j * H_SUB, (j + 1) * H_SUB)
        qb, kb, vb, bb = q[sl], k[sl], v[sl], b[sl]
        od = jnp.zeros((H_SUB, H_EXPAND), F32)
        for s in range(H_SUB):
            e = jnp.exp(jnp.minimum(bb - bb[s:s + 1, :], 0.0))
            col = jnp.sum(qb * (kb[s:s + 1, :] * e), axis=-1, keepdims=True)
            od = od + jnp.where(t_idx >= s, col, 0.0) * vb[s:s + 1, :]
        diag.append(od)
    return o + jnp.concatenate(diag, axis=0), st_new


def _hgrn_kernel(q_ref, k_ref, v_ref, lf_ref, g_ref, ng_ref, o_ref, st_ref, *, n_chunks):
    @pl.when(pl.program_id(2) == 0)
    def _():
        st_ref[...] = jnp.zeros(st_ref.shape, F32)

    def body(ci, carry):
        sl = pl.ds(pl.multiple_of(ci * H_CHUNK, H_CHUNK), H_CHUNK)
        o, st_new = _hgrn_chunk(q_ref[sl, :], k_ref[sl, :], v_ref[sl, :], lf_ref[sl, :], st_ref[...])
        st_ref[...] = st_new
        o = o * lax.rsqrt(jnp.mean(o * o, axis=-1, keepdims=True) + LN_EPS) * ng_ref[...]
        o_ref[sl, :] = (o * g_ref[sl, :]).astype(o_ref.dtype)
        return carry

    lax.fori_loop(0, n_chunks, body, 0)


def _hgrn_recurrence(q, k, v, lf, g, ng, bsz, seq, tc):
    t = q.shape[0]
    nt = seq // tc
    blk = pl.BlockSpec((tc, H_EXPAND), lambda b, h, c: (b * nt + c, h))
    return pl.pallas_call(
        functools.partial(_hgrn_kernel, n_chunks=tc // H_CHUNK),
        grid=(bsz, H_HEADS, nt),
        in_specs=[blk] * 5 + [pl.BlockSpec((1, H_EXPAND), lambda b, h, c: (0, 0))],
        out_specs=blk,
        out_shape=jax.ShapeDtypeStruct((t, D_MODEL), BF16),
        scratch_shapes=[pltpu.VMEM((H_EXPAND, H_EXPAND), F32)],
        compiler_params=_cparams(("arbitrary", "arbitrary", "arbitrary")),
        name="hgrn_recurrence",
    )(q, k, v, lf, g, ng)


def _hgrn_layer(x2, bsz, seq, w_in, w_out, lb, norm_g, ln_g, ln_b, tm=256, tc=512):
    q, k, v, lf, g = _hgrn_proj(x2, w_in.astype(BF16), lb.reshape(1, -1).astype(F32), tm)
    o = _hgrn_recurrence(q, k, v, lf, g, norm_g.reshape(1, -1).astype(F32), bsz, seq, tc)
    return _out_proj(o, x2, w_out.astype(BF16), ln_g.reshape(1, -1), ln_b.reshape(1, -1), 2 * tm)


GROUP_SIZE = N_EXPERTS // N_GROUPS
X_WORDS = D_MODEL // 2
X_SLAB = X_WORDS // LANES
Y_SLAB = D_MODEL // LANES


def _router_kernel(x_ref, wt_ref, bias_ref, idx_ref, rank_ref, gate_tm_ref, cnt_ref,
                   carry_ref, *, tr):
    @pl.when(pl.program_id(0) == 0)
    def _():
        carry_ref[...] = jnp.zeros(carry_ref.shape, F32)

    e_n = N_EXPERTS
    logits = lax.dot_general(wt_ref[...], x_ref[...], (((1,), (1,)), ((), ())),
                             precision=lax.Precision.HIGHEST, preferred_element_type=F32)
    scores = jax.nn.sigmoid(logits)
    choice = scores + bias_ref[...]

    ch3 = choice.reshape(N_GROUPS, GROUP_SIZE, tr)
    sub = lax.broadcasted_iota(jnp.int32, ch3.shape, 1)
    m1 = jnp.max(ch3, axis=1, keepdims=True)
    first = jnp.min(jnp.where(ch3 == m1, sub, GROUP_SIZE), axis=1, keepdims=True)
    m2 = jnp.max(jnp.where(sub == first, -jnp.inf, ch3), axis=1, keepdims=True)
    gs = (m1 + m2).reshape(N_GROUPS, tr)

    g_i = lax.broadcasted_iota(jnp.int32, gs.shape, 0)
    g_rank = jnp.zeros(gs.shape, F32)
    for g in range(N_GROUPS):
        row = gs[g:g + 1, :]
        ahead = (row > gs) | ((row == gs) & (g_i > g))
        g_rank = g_rank + jnp.where(ahead, 1.0, 0.0)
    g_keep = (g_rank < TOPK_GROUP).astype(F32).reshape(N_GROUPS, 1, tr)
    keep = jnp.broadcast_to(g_keep, (N_GROUPS, GROUP_SIZE, tr)).reshape(e_n, tr) > 0.5
    cm = jnp.where(keep, choice, -jnp.inf)

    e_i = lax.broadcasted_iota(jnp.int32, cm.shape, 0)
    e_rank = jnp.zeros(cm.shape, F32)
    for e in range(e_n):
        row = cm[e:e + 1, :]
        ahead = (row > cm) | ((row == cm) & (e_i > e))
        e_rank = e_rank + jnp.where(ahead, 1.0, 0.0)
    sel = e_rank < TOP_K
    sel_f = sel.astype(F32)
    w = jnp.where(sel, scores, 0.0)
    gate = w / (jnp.sum(w, axis=0, keepdims=True) + 1e-20) * ROUTED_SCALE

    sel_b = sel_f.astype(BF16)
    t_r = lax.broadcasted_iota(jnp.int32, (tr, tr), 0)
    t_c = lax.broadcasted_iota(jnp.int32, (tr, tr), 1)
    before = jnp.dot(sel_b, (t_r < t_c).astype(BF16), preferred_element_type=F32)
    tok_rank = carry_ref[:, 0:1] + before
    carry_ref[...] = carry_ref[...] + jnp.sum(sel_f, axis=1, keepdims=True)
    cnt_ref[...] = carry_ref[...]

    x_r = lax.broadcasted_iota(jnp.int32, (e_n, e_n), 0)
    x_c = lax.broadcasted_iota(jnp.int32, (e_n, e_n), 1)
    slot = jnp.dot((x_c < x_r).astype(BF16), sel_b, preferred_element_type=F32)
    e_f = e_i.astype(F32)
    idx_rows, gate_rows, rank_rows = [], [], []
    for j in range(TOP_K):
        pick = jnp.where(sel & (slot == j), 1.0, 0.0)
        idx_rows.append(jnp.sum(pick * e_f, axis=0, keepdims=True))
        gate_rows.append(jnp.sum(pick * gate, axis=0, keepdims=True))
        rank_rows.append(jnp.sum(pick * tok_rank, axis=0, keepdims=True))
    idx_ref[...] = jnp.concatenate(idx_rows, axis=0).astype(jnp.int32)
    rank_ref[...] = jnp.concatenate(rank_rows, axis=0).astype(jnp.int32)
    gates = jnp.concatenate(gate_rows, axis=0)
    padded = jnp.concatenate([gates, jnp.zeros((LANES - TOP_K, tr), F32)], axis=0)
    gate_tm_ref[...] = padded.T


def _router(x2, wt, bias, tr):
    t = x2.shape[0]
    col = lambda i: (0, i)
    return pl.pallas_call(
        functools.partial(_router_kernel, tr=tr),
        grid=(t // tr,),
        in_specs=[pl.BlockSpec((tr, D_MODEL), lambda i: (i, 0)),
                  pl.BlockSpec((N_EXPERTS, D_MODEL), lambda i: (0, 0)),
                  pl.BlockSpec((N_EXPERTS, 1), lambda i: (0, 0))],
        out_specs=[pl.BlockSpec((TOP_K, tr), col),
                   pl.BlockSpec((TOP_K, tr), col), pl.BlockSpec((tr, LANES), lambda i: (i, 0)),
                   pl.BlockSpec((N_EXPERTS, LANES), lambda i: (0, 0))],
        out_shape=[jax.ShapeDtypeStruct((TOP_K, t), jnp.int32),
                   jax.ShapeDtypeStruct((TOP_K, t), jnp.int32),
                   jax.ShapeDtypeStruct((t, LANES), F32),
                   jax.ShapeDtypeStruct((N_EXPERTS, LANES), F32)],
        scratch_shapes=[pltpu.VMEM((N_EXPERTS, LANES), F32)],
        compiler_params=_cparams(("arbitrary",)),
        name="moe_router",
    )(x2, wt, bias)


def _slots_kernel(start_ref, idx_ref, rank_ref, dest_ref):
    idx = idx_ref[...]
    dest = rank_ref[...]
    for e in range(N_EXPERTS):
        dest = dest + jnp.where(idx == e, start_ref[e], 0)
    dest_ref[...] = dest


def _slots(pad_start, idx, rank, tl):
    t = idx.shape[1]
    col = pl.BlockSpec((TOP_K, tl), lambda i: (0, i))
    return pl.pallas_call(
        _slots_kernel,
        grid=(t // tl,),
        in_specs=[pl.BlockSpec(memory_space=pltpu.SMEM), col, col],
        out_specs=col,
        out_shape=jax.ShapeDtypeStruct(idx.shape, jnp.int32),
        compiler_params=_cparams(("arbitrary",)),
        name="moe_slots",
    )(pad_start, idx, rank)


def _dispatch_kernel(dest_ref, x_ref, zeros_ref, xs_ref, xp_ref, sem, *, td):
    del zeros_ref
    x = x_ref[...]
    lo = pltpu.bitcast(x[:, :X_WORDS].astype(BF16).astype(F32), jnp.uint32)
    hi = pltpu.bitcast(x[:, X_WORDS:].astype(BF16).astype(F32), jnp.uint32)
    words = (lo >> 16) | (hi & jnp.uint32(0xFFFF0000))
    for j in range(X_SLAB):
        xp_ref[:, j, :] = words[:, j * LANES:(j + 1) * LANES]

    def issue(t, carry):
        for j in range(TOP_K):
            pltpu.make_async_copy(xp_ref.at[t], xs_ref.at[dest_ref[j, t]], sem).start(priority=j % 2)
        return carry

    lax.fori_loop(0, td, issue, 0)
    for j in range(TOP_K):
        pltpu.make_async_copy(xp_ref, xs_ref.at[pl.ds(0, td)], sem).wait()


def _dispatch(dest, x2, zeros, td):
    t = x2.shape[0]
    return pl.pallas_call(
        functools.partial(_dispatch_kernel, td=td),
        grid=(t // td,),
        in_specs=[pl.BlockSpec((TOP_K, td), lambda i: (0, i), memory_space=pltpu.SMEM),
                  pl.BlockSpec((td, D_MODEL), lambda i: (i, 0)),
                  pl.BlockSpec(memory_space=pl.ANY)],
        out_specs=pl.BlockSpec(memory_space=pl.ANY),
        out_shape=jax.ShapeDtypeStruct(zeros.shape, zeros.dtype),
        input_output_aliases={2: 0},
        scratch_shapes=[pltpu.VMEM((td, X_SLAB, LANES), jnp.uint32), pltpu.SemaphoreType.DMA],
        compiler_params=_cparams(("arbitrary",)),
        name="moe_dispatch",
    )(dest, x2, zeros)


def _unpack_rows(words):
    lo = pltpu.bitcast(words << 16, F32)
    hi = pltpu.bitcast(words & jnp.uint32(0xFFFF0000), F32)
    return jnp.concatenate([lo, hi], axis=1).astype(BF16)


def _swiglu(xb, w_gu, w_d):
    h = jnp.dot(xb, w_gu, preferred_element_type=F32)
    act = h[:, :EXPERT_FF] * jax.nn.sigmoid(h[:, :EXPERT_FF]) * h[:, EXPERT_FF:]
    return jnp.dot(act.astype(BF16), w_d, preferred_element_type=F32)


def _expert_kernel(blk_ref, exp_ref, used_ref, xs_ref, wgu_ref, wd_ref, ys_ref, *, te):
    del blk_ref, exp_ref
    live = pl.program_id(0) < used_ref[0]

    @pl.when(live)
    def _():
        words = jnp.concatenate(
            [xs_ref[pl.ds(j, te, stride=X_SLAB), :] for j in range(X_SLAB)], axis=1)
        y = _swiglu(_unpack_rows(words), wgu_ref[0], wd_ref[0])
        for c in range(Y_SLAB):
            ys_ref[pl.ds(c, te, stride=Y_SLAB), :] = y[:, c * LANES:(c + 1) * LANES]

    @pl.when(jnp.logical_not(live))
    def _():
        ys_ref[...] = jnp.zeros(ys_ref.shape, ys_ref.dtype)


def _experts(blk, exp, used, xs2, w_gu, w_d, te):
    n_blocks = blk.shape[0]
    n_rows = xs2.shape[0] // X_SLAB
    return pl.pallas_call(
        functools.partial(_expert_kernel, te=te),
        grid_spec=pltpu.PrefetchScalarGridSpec(
            num_scalar_prefetch=3,
            grid=(n_blocks,),
            in_specs=[pl.BlockSpec((te * X_SLAB, LANES), lambda b, blk, exp, used: (blk[b], 0)),
                      pl.BlockSpec((1, D_MODEL, 2 * EXPERT_FF), lambda b, blk, exp, used: (exp[b], 0, 0)),
                      pl.BlockSpec((1, EXPERT_FF, D_MODEL), lambda b, blk, exp, used: (exp[b], 0, 0))],
            out_specs=pl.BlockSpec((te * Y_SLAB, LANES), lambda b, blk, exp, used: (b, 0)),
        ),
        out_shape=jax.ShapeDtypeStruct((n_rows * Y_SLAB, LANES), F32),
        compiler_params=_cparams(("arbitrary",)),
        name="moe_experts",
    )(blk, exp, used, xs2, w_gu, w_d)


def _combine_kernel(dest_ref, dest_next_ref, gate_ref, x_ref, wgu_ref, wd_ref, g_ref, b_ref, ys_ref,
                    o_ref, ybuf_ref, sem, *, tc):
    i = pl.program_id(0)
    n = pl.num_programs(0)
    slot = i % 2
    rows = TOP_K * tc

    def row_copy(src_row, dst_row, into):
        src = ys_ref.at[pl.ds(pl.multiple_of(src_row * Y_SLAB, Y_SLAB), Y_SLAB)]
        dst = ybuf_ref.at[pl.ds(pl.multiple_of(dst_row * Y_SLAB, Y_SLAB), Y_SLAB)]
        return pltpu.make_async_copy(src, dst, sem.at[into])

    def gather(d_ref, into):
        def issue(t, carry):
            for j in range(TOP_K):
                row_copy(d_ref[j, t], into * rows + j * tc + t, into).start(priority=j % 2)
            return carry
        lax.fori_loop(0, tc, issue, 0)

    @pl.when(i == 0)
    def _():
        gather(dest_ref, 0)

    @pl.when(i + 1 < n)
    def _():
        gather(dest_next_ref, 1 - slot)

    x = x_ref[...]
    acc = ALPHA * x + _swiglu(x.astype(BF16), wgu_ref[...], wd_ref[...])

    base = pl.multiple_of(slot * rows * Y_SLAB, rows * Y_SLAB)
    pltpu.make_async_copy(ys_ref.at[pl.ds(0, rows * Y_SLAB)],
                          ybuf_ref.at[pl.ds(base, rows * Y_SLAB)], sem.at[slot]).wait()
    gate = gate_ref[...]
    for j in range(TOP_K):
        y_j = jnp.concatenate(
            [ybuf_ref[pl.ds(base + j * tc * Y_SLAB + c, tc, stride=Y_SLAB), :] for c in range(Y_SLAB)],
            axis=1)
        acc = acc + gate[:, j:j + 1] * y_j
    o_ref[...] = _layer_norm(acc, g_ref[...], b_ref[...])


def _combine(dest, gate_tm, x2, ys, ws_gu, ws_d, g, b, tc):
    t = x2.shape[0]
    n = t // tc
    fixed = lambda i: (0, 0)
    row = lambda i: (i, 0)
    smem = lambda f: pl.BlockSpec((TOP_K, tc), f, memory_space=pltpu.SMEM)
    return pl.pallas_call(
        functools.partial(_combine_kernel, tc=tc),
        grid=(n,),
        in_specs=[smem(lambda i: (0, i)), smem(lambda i: (0, jnp.minimum(i + 1, n - 1))),
                  pl.BlockSpec((tc, LANES), row), pl.BlockSpec((tc, D_MODEL), row),
                  pl.BlockSpec((D_MODEL, 2 * EXPERT_FF), fixed), pl.BlockSpec((EXPERT_FF, D_MODEL), fixed),
                  pl.BlockSpec((1, D_MODEL), fixed), pl.BlockSpec((1, D_MODEL), fixed),
                  pl.BlockSpec(memory_space=pl.ANY)],
        out_specs=pl.BlockSpec((tc, D_MODEL), row),
        out_shape=jax.ShapeDtypeStruct((t, D_MODEL), F32),
        scratch_shapes=[pltpu.VMEM((2 * TOP_K * tc * Y_SLAB, LANES), F32),
                        pltpu.SemaphoreType.DMA((2,))],
        compiler_params=_cparams(("arbitrary",)),
        name="moe_combine",
    )(dest, dest, gate_tm, x2, ws_gu, ws_d, g, b, ys)


def _moe_layer(x2, w_router, bias, w_gu, w_d, ws_gu, ws_d, ln_g, ln_b, te=512, tr=512, td=256, tc=128):
    t = x2.shape[0]
    n_blocks = t * TOP_K // te + N_EXPERTS
    n_rows = n_blocks * te
    idx, rank, gate_tm, cnt = _router(x2, w_router.T.astype(F32), bias.reshape(-1, 1).astype(F32), tr)

    counts = cnt[:, 0].astype(jnp.int32)
    padded = (counts + te - 1) // te * te
    pad_end = jnp.cumsum(padded)
    pad_start = pad_end - padded
    dest = _slots(pad_start, idx, rank, min(t, 4096))
    used = pad_end[-1] // te
    blk = jnp.minimum(jnp.arange(n_blocks, dtype=jnp.int32), used - 1)
    exp = jnp.sum((pad_end[None, :] // te <= blk[:, None]).astype(jnp.int32), axis=1)
    exp = jnp.minimum(exp, N_EXPERTS - 1)

    xs = _dispatch(dest, x2, jnp.zeros((n_rows, X_SLAB, LANES), jnp.uint32), td)
    ys = _experts(blk, exp.astype(jnp.int32), used.reshape(1), xs.reshape(n_rows * X_SLAB, LANES),
                  w_gu.astype(BF16), w_d.astype(BF16), te)
    return _combine(dest, gate_tm, x2, ys, ws_gu.astype(BF16), ws_d.astype(BF16),
                    ln_g.reshape(1, -1), ln_b.reshape(1, -1), tc)


def kernel(x, attn_w_in, attn_w_out, attn_lambda, attn_subln, hgrn_w_in, hgrn_w_out, hgrn_lower_bounds, hgrn_norm, moe_router, moe_router_bias, moe_w_gate_up, moe_w_down, shared_w_gate_up, shared_w_down, ln_gain, ln_bias):
    bsz, seq, _ = x.shape
    tables = _rotary_tables(seq)
    lb = jax.nn.softmax(hgrn_lower_bounds.astype(F32), axis=0)
    lb = jnp.cumsum(lb, axis=0) - lb[0]
    x2 = x.reshape(bsz * seq, D_MODEL)
    for layer in range(DEPTH):
        j = layer // 2
        if layer % 2 == 0:
            lambda_init = 0.8 - 0.6 * math.exp(-0.3 * layer)
            x2 = _attn_layer(x2, bsz, seq, attn_w_in[j], attn_w_out[j], attn_lambda[j], attn_subln[j],
                             lambda_init, ln_gain[layer, 0], ln_bias[layer, 0], tables)
        else:
            x2 = _hgrn_layer(x2, bsz, seq, hgrn_w_in[j], hgrn_w_out[j], lb[layer], hgrn_norm[j],
                             ln_gain[layer, 0], ln_bias[layer, 0])
        x2 = _moe_layer(x2, moe_router[layer], moe_router_bias[layer], moe_w_gate_up[layer],
                        moe_w_down[layer], shared_w_gate_up[layer], shared_w_down[layer],
                        ln_gain[layer, 1], ln_bias[layer, 1])
    return x2.reshape(bsz, seq, D_MODEL)
```

```python
import functools
import math

import jax
import jax.numpy as jnp
from jax import lax
from jax.experimental import pallas as pl
from jax.experimental.pallas import tpu as pltpu

F32 = jnp.float32
BF16 = jnp.bfloat16

D_MODEL = 1024
DEPTH = 4
CHUNK = 64
A_HEADS = 8
A_HEAD_DIM = 64
ROT_DIM = 16
ROPE_THETA = 500000.0
H_HEADS = 8
H_EXPAND = 128
N_EXPERTS = 64
N_GROUPS = 8
TOPK_GROUP = 4
TOP_K = 8
EXPERT_FF = 256
ROUTED_SCALE = 2.5
ALPHA = (2 * DEPTH) ** 0.25
LN_EPS = 1e-5
LANES = 128
VMEM_LIMIT = 48 * 1024 * 1024


def _cparams(sem):
    return pltpu.CompilerParams(dimension_semantics=sem, vmem_limit_bytes=VMEM_LIMIT)


def _layer_norm(y, g, b):
    mu = jnp.mean(y, axis=-1, keepdims=True)
    yc = y - mu
    var = jnp.mean(yc * yc, axis=-1, keepdims=True)
    return yc * lax.rsqrt(var + LN_EPS) * g + b


def _dot_nt(a, b):
    return lax.dot_general(a, b, (((1,), (1,)), ((), ())), preferred_element_type=F32)


def _dot_tn(a, b):
    return lax.dot_general(a, b, (((0,), (0,)), ((), ())), preferred_element_type=F32)


def _attn_proj_kernel(x_ref, w_ref, cf_ref, s1_ref, s2_ref, q_ref, k_ref, v_ref):
    xb = x_ref[...].astype(BF16)
    cf, s1, s2 = cf_ref[...], s1_ref[...], s2_ref[...]

    def rotary(y):
        outs = []
        for c in range(D_MODEL // LANES):
            yc = y[:, c * LANES:(c + 1) * LANES]
            outs.append(yc * cf + pltpu.roll(yc, LANES - ROT_DIM // 2, 1) * s1
                        + pltpu.roll(yc, ROT_DIM // 2, 1) * s2)
        return jnp.concatenate(outs, axis=1)

    q = jnp.dot(xb, w_ref[:, 0:D_MODEL], preferred_element_type=F32)
    q_ref[...] = (rotary(q) * (A_HEAD_DIM ** -0.5 * math.log2(math.e))).astype(BF16)
    k = jnp.dot(xb, w_ref[:, D_MODEL:2 * D_MODEL], preferred_element_type=F32)
    k_ref[...] = rotary(k).astype(BF16)
    v = jnp.dot(xb, w_ref[:, 2 * D_MODEL:3 * D_MODEL], preferred_element_type=F32)
    v_ref[...] = v.astype(BF16)


def _attn_proj(x2, w, cf, s1, s2, seq, tm):
    t = x2.shape[0]
    nseq = seq // tm
    row = lambda i: (i, 0)
    tab = lambda i: (i % nseq, 0)
    out = jax.ShapeDtypeStruct((t, D_MODEL), BF16)
    return pl.pallas_call(
        _attn_proj_kernel,
        grid=(t // tm,),
        in_specs=[pl.BlockSpec((tm, D_MODEL), row),
                  pl.BlockSpec((D_MODEL, 3 * D_MODEL), lambda i: (0, 0)),
                  pl.BlockSpec((tm, LANES), tab), pl.BlockSpec((tm, LANES), tab),
                  pl.BlockSpec((tm, LANES), tab)],
        out_specs=[pl.BlockSpec((tm, D_MODEL), row)] * 3,
        out_shape=[out, out, out],
        compiler_params=_cparams(("arbitrary",)),
        name="attn_proj",
    )(x2, w, cf, s1, s2)


def _attn_kernel(lam_ref, q_ref, k_ref, v_ref, g_ref, o_ref,
                 m1_ref, l1_ref, a1_ref, m2_ref, l2_ref, a2_ref, *, tq, out_scale):
    i = pl.program_id(2)
    q = q_ref[0]
    lane = lax.broadcasted_iota(jnp.int32, q.shape, 1)
    zero = jnp.zeros_like(q)
    q1 = jnp.where(lane < A_HEAD_DIM, q, zero)
    q2 = jnp.where(lane >= A_HEAD_DIM, q, zero)

    for m_ref, l_ref, a_ref in ((m1_ref, l1_ref, a1_ref), (m2_ref, l2_ref, a2_ref)):
        m_ref[...] = jnp.full(m_ref.shape, -jnp.inf, F32)
        l_ref[...] = jnp.zeros(l_ref.shape, F32)
        a_ref[...] = jnp.zeros(a_ref.shape, F32)

    def update(s, vb, m_ref, l_ref, a_ref):
        tiles = [s[:, c * LANES:(c + 1) * LANES] for c in range(s.shape[1] // LANES)]
        mc = functools.reduce(jnp.maximum, tiles)
        m_old = m_ref[...]
        m_new = jnp.maximum(m_old, jnp.max(mc, axis=-1, keepdims=True))
        alpha = jnp.exp2(m_old - m_new)
        p = [jnp.exp2(t - m_new) for t in tiles]
        l_ref[...] = alpha * l_ref[...] + functools.reduce(jnp.add, p)
        pb = jnp.concatenate([t.astype(BF16) for t in p], axis=1)
        a_ref[...] = alpha * a_ref[...] + jnp.dot(pb, vb, preferred_element_type=F32)
        m_ref[...] = m_new

    def block(j, mask):
        start = pl.multiple_of(j * tq, tq)
        kb = k_ref[0, pl.ds(start, tq), :]
        vb = v_ref[0, pl.ds(start, tq), :]
        s1 = _dot_nt(q1, kb)
        s2 = _dot_nt(q2, kb)
        if mask is not None:
            s1 = jnp.where(mask, s1, -jnp.inf)
            s2 = jnp.where(mask, s2, -jnp.inf)
        update(s1, vb, m1_ref, l1_ref, a1_ref)
        update(s2, vb, m2_ref, l2_ref, a2_ref)

    def full_block(j, carry):
        block(j, None)
        return carry

    lax.fori_loop(0, i, full_block, 0)
    rq = lax.broadcasted_iota(jnp.int32, (tq, tq), 0) // CHUNK
    ck = lax.broadcasted_iota(jnp.int32, (tq, tq), 1) // CHUNK
    block(i, ck <= rq)


    lam = lam_ref[0]
    l1 = jnp.sum(l1_ref[...], axis=-1, keepdims=True)
    l2 = jnp.sum(l2_ref[...], axis=-1, keepdims=True)
    o = a1_ref[...] / l1 - lam * (a2_ref[...] / l2)
    o = o * lax.rsqrt(jnp.mean(o * o, axis=-1, keepdims=True) + LN_EPS) * g_ref[...]
    o_ref[0] = (o * out_scale).astype(o_ref.dtype)


def _attention(lam, q, k, v, g, lambda_init, tq):
    b, s, _ = q.shape
    kern = functools.partial(_attn_kernel, tq=tq, out_scale=1.0 - lambda_init)
    kv_spec = pl.BlockSpec((1, s, 2 * A_HEAD_DIM), lambda bi, h, i: (bi, 0, h))
    qo_spec = pl.BlockSpec((1, tq, 2 * A_HEAD_DIM), lambda bi, h, i: (bi, i, h))
    vec = lambda n: pltpu.VMEM((tq, n), F32)
    return pl.pallas_call(
        kern,
        grid=(b, A_HEADS, s // tq),
        in_specs=[pl.BlockSpec(memory_space=pltpu.SMEM), qo_spec, kv_spec, kv_spec,
                  pl.BlockSpec((1, 2 * A_HEAD_DIM), lambda bi, h, i: (0, 0))],
        out_specs=qo_spec,
        out_shape=jax.ShapeDtypeStruct((b, s, D_MODEL), BF16),
        scratch_shapes=[vec(LANES)] * 6,
        compiler_params=_cparams(("arbitrary", "arbitrary", "arbitrary")),
        name="diff_attention",
    )(lam, q, k, v, g)


def _out_proj_kernel(o_ref, x_ref, w_ref, g_ref, b_ref, y_ref):
    h = jnp.dot(o_ref[...], w_ref[...], preferred_element_type=F32)
    y_ref[...] = _layer_norm(ALPHA * x_ref[...] + h, g_ref[...], b_ref[...])


def _out_proj(o2, x2, w, g, b, tm):
    t = x2.shape[0]
    row = lambda i: (i, 0)
    fixed = lambda i: (0, 0)
    return pl.pallas_call(
        _out_proj_kernel,
        grid=(t // tm,),
        in_specs=[pl.BlockSpec((tm, D_MODEL), row), pl.BlockSpec((tm, D_MODEL), row),
                  pl.BlockSpec((D_MODEL, D_MODEL), fixed),
                  pl.BlockSpec((1, D_MODEL), fixed), pl.BlockSpec((1, D_MODEL), fixed)],
        out_specs=pl.BlockSpec((tm, D_MODEL), row),
        out_shape=jax.ShapeDtypeStruct((t, D_MODEL), F32),
        compiler_params=_cparams(("arbitrary",)),
        name="out_proj_ln",
    )(o2, x2, w, g, b)


def _rotary_tables(seq):
    pos = jnp.arange(seq, dtype=F32)
    inv_freq = ROPE_THETA ** (-jnp.arange(0, ROT_DIM, 2, dtype=F32) / ROT_DIM)
    ang = pos[:, None] * inv_freq[None, :]
    cos, sin = jnp.cos(ang), jnp.sin(ang)
    half = ROT_DIM // 2
    pad = A_HEAD_DIM - ROT_DIM
    one = jnp.ones((seq, pad), F32)
    zero = jnp.zeros((seq, pad), F32)
    zh = jnp.zeros((seq, half), F32)
    cf = jnp.concatenate([cos, cos, one], axis=1)
    s1 = jnp.concatenate([-sin, zh, zero], axis=1)
    s2 = jnp.concatenate([zh, sin, zero], axis=1)
    rep = LANES // A_HEAD_DIM
    return tuple(jnp.tile(t, (1, rep)) for t in (cf, s1, s2))


def _attn_layer(x2, bsz, seq, w_in, w_out, lam_params, subln, lambda_init, ln_g, ln_b, tables,
                tm=512, tq=512):
    q, k, v = _attn_proj(x2, w_in.astype(BF16), *tables, seq, tm)
    lp = lam_params.astype(F32)
    lam = jnp.exp(jnp.sum(lp[0] * lp[1])) - jnp.exp(jnp.sum(lp[2] * lp[3])) + lambda_init
    shp = (bsz, seq, D_MODEL)
    o = _attention(lam.reshape(1), q.reshape(shp), k.reshape(shp), v.reshape(shp),
                   subln.reshape(1, -1).astype(F32), lambda_init, tq)
    return _out_proj(o.reshape(bsz * seq, D_MODEL), x2, w_out.astype(BF16),
                     ln_g.reshape(1, -1), ln_b.reshape(1, -1), tm)


H_CHUNK = 128
H_SUB = 16


def _hgrn_proj_kernel(x_ref, w_ref, lb_ref, q_ref, k_ref, v_ref, lf_ref, g_ref):
    xb = x_ref[...].astype(BF16)
    w = H_HEADS * H_EXPAND
    q = jnp.dot(xb, w_ref[:, 0:w], preferred_element_type=F32)
    q_ref[...] = q * jax.nn.sigmoid(q)
    f = jnp.dot(xb, w_ref[:, w:2 * w], preferred_element_type=F32)
    lb = lb_ref[...]
    forget = lb + (1.0 - lb) * jax.nn.sigmoid(f)
    lf_ref[...] = jnp.log(forget)
    k_ref[...] = 1.0 - forget
    v_ref[...] = jnp.dot(xb, w_ref[:, 2 * w:3 * w], preferred_element_type=F32)
    g = jnp.dot(xb, w_ref[:, 3 * w:4 * w], preferred_element_type=F32)
    g_ref[...] = g * jax.nn.sigmoid(g)


def _hgrn_proj(x2, w, lb, tm):
    t = x2.shape[0]
    row = lambda i: (i, 0)
    fixed = lambda i: (0, 0)
    out = jax.ShapeDtypeStruct((t, D_MODEL), F32)
    return pl.pallas_call(
        _hgrn_proj_kernel,
        grid=(t // tm,),
        in_specs=[pl.BlockSpec((tm, D_MODEL), row), pl.BlockSpec((D_MODEL, 4 * D_MODEL), fixed),
                  pl.BlockSpec((1, D_MODEL), fixed)],
        out_specs=[pl.BlockSpec((tm, D_MODEL), row)] * 5,
        out_shape=[out] * 5,
        compiler_params=_cparams(("arbitrary",)),
        name="hgrn_proj",
    )(x2, w, lb)


def _hgrn_chunk(q, k, v, lf, st):
    c, nb = H_CHUNK, H_CHUNK // H_SUB
    r_i = lax.broadcasted_iota(jnp.int32, (c, c), 0)
    c_i = lax.broadcasted_iota(jnp.int32, (c, c), 1)
    tri = (r_i >= c_i).astype(BF16)
    hi = lf.astype(BF16)
    lo = (lf - hi.astype(F32)).astype(BF16)
    b2 = jnp.dot(tri, jnp.concatenate([hi, lo], axis=1), preferred_element_type=F32)
    b = (b2[:, :H_EXPAND] + b2[:, H_EXPAND:]) * math.log2(math.e)
    b_last = b[c - 1:c, :]

    o = _dot_nt((q * jnp.exp2(b)).astype(BF16), st.astype(BF16))
    ke = k * jnp.exp2(b_last - b)
    st_new = st * jnp.exp2(b_last) + _dot_tn(v.astype(BF16), ke.astype(BF16))

    ends = [b[(j + 1) * H_SUB - 1:(j + 1) * H_SUB, :] for j in range(nb)]
    e_blk = jnp.concatenate([jnp.broadcast_to(e, (H_SUB, H_EXPAND)) for e in ends], axis=0)
    kt = k * jnp.exp2(e_blk - b)
    row_blk = lax.broadcasted_iota(jnp.int32, (c, H_EXPAND), 0) // H_SUB
    q_big = jnp.concatenate(
        [(q * jnp.exp2(jnp.minimum(b - ends[j], 0.0))).astype(BF16) for j in range(nb)], axis=1)
    k_big = jnp.concatenate(
        [jnp.where(row_blk == j, kt, 0.0).astype(BF16) for j in range(nb)], axis=1)
    a = _dot_nt(q_big, k_big)
    a = jnp.where(r_i // H_SUB > c_i // H_SUB, a, 0.0)
    o = o + jnp.dot(a.astype(BF16), v.astype(BF16), preferred_element_type=F32)

    t_idx = lax.broadcasted_iota(jnp.int32, (H_SUB, 1), 0)
    diag = []
    for j in range(nb):
        sl = slice(j * H_SUB, (j + 1) * H_SUB)
        qb, kb, vb, bb = q[sl], k[sl], v[sl], b[sl]
        od = jnp.zeros((H_SUB, H_EXPAND), F32)
        for s in range(H_SUB):
            e = jnp.exp2(bb - bb[s:s + 1, :])
            col = jnp.sum(qb * (kb[s:s + 1, :] * e), axis=-1, keepdims=True)
            od = od + jnp.where(t_idx >= s, col, 0.0) * vb[s:s + 1, :]
        diag.append(od)
    return o + jnp.concatenate(diag, axis=0), st_new


def _hgrn_kernel(q_ref, k_ref, v_ref, lf_ref, g_ref, ng_ref, o_ref, st_ref, *, n_chunks):
    @pl.when(pl.program_id(2) == 0)
    def _():
        st_ref[...] = jnp.zeros(st_ref.shape, F32)

    st = st_ref[...]
    for ci in range(n_chunks):
        sl = slice(ci * H_CHUNK, (ci + 1) * H_CHUNK)
        o, st = _hgrn_chunk(q_ref[sl, :], k_ref[sl, :], v_ref[sl, :], lf_ref[sl, :], st)
        o = o * lax.rsqrt(jnp.mean(o * o, axis=-1, keepdims=True) + LN_EPS) * ng_ref[...]
        o_ref[sl, :] = (o * g_ref[sl, :]).astype(o_ref.dtype)
    st_ref[...] = st


def _hgrn_recurrence(q, k, v, lf, g, ng, bsz, seq, tc):
    t = q.shape[0]
    nt = seq // tc
    blk = pl.BlockSpec((tc, H_EXPAND), lambda b, h, c: (b * nt + c, h))
    return pl.pallas_call(
        functools.partial(_hgrn_kernel, n_chunks=tc // H_CHUNK),
        grid=(bsz, H_HEADS, nt),
        in_specs=[blk] * 5 + [pl.BlockSpec((1, H_EXPAND), lambda b, h, c: (0, 0))],
        out_specs=blk,
        out_shape=jax.ShapeDtypeStruct((t, D_MODEL), BF16),
        scratch_shapes=[pltpu.VMEM((H_EXPAND, H_EXPAND), F32)],
        compiler_params=_cparams(("arbitrary", "arbitrary", "arbitrary")),
        name="hgrn_recurrence",
    )(q, k, v, lf, g, ng)


def _hgrn_layer(x2, bsz, seq, w_in, w_out, lb, norm_g, ln_g, ln_b, tm=256, tc=512):
    q, k, v, lf, g = _hgrn_proj(x2, w_in.astype(BF16), lb.reshape(1, -1).astype(F32), tm)
    o = _hgrn_recurrence(q, k, v, lf, g, norm_g.reshape(1, -1).astype(F32), bsz, seq, tc)
    return _out_proj(o, x2, w_out.astype(BF16), ln_g.reshape(1, -1), ln_b.reshape(1, -1), 2 * tm)


GROUP_SIZE = N_EXPERTS // N_GROUPS
X_WORDS = D_MODEL // 2
X_SLAB = X_WORDS // LANES


def _router_kernel(x_ref, wt_ref, bias_ref, idx_ref, rank_ref, gate_tm_ref, cnt_ref,
                   carry_ref, *, tr):
    @pl.when(pl.program_id(0) == 0)
    def _():
        carry_ref[...] = jnp.zeros(carry_ref.shape, F32)

    e_n = N_EXPERTS
    logits = lax.dot_general(wt_ref[...], x_ref[...], (((1,), (1,)), ((), ())),
                             precision=lax.Precision.HIGHEST, preferred_element_type=F32)
    scores = jax.nn.sigmoid(logits)
    choice = scores + bias_ref[...]

    ch3 = choice.reshape(N_GROUPS, GROUP_SIZE, tr)
    sub = lax.broadcasted_iota(jnp.int32, ch3.shape, 1)
    m1 = jnp.max(ch3, axis=1, keepdims=True)
    first = jnp.min(jnp.where(ch3 == m1, sub, GROUP_SIZE), axis=1, keepdims=True)
    m2 = jnp.max(jnp.where(sub == first, -jnp.inf, ch3), axis=1, keepdims=True)
    gs = (m1 + m2).reshape(N_GROUPS, tr)

    g_i = lax.broadcasted_iota(jnp.int32, gs.shape, 0)
    g_rank = jnp.zeros(gs.shape, F32)
    for g in range(N_GROUPS):
        row = gs[g:g + 1, :]
        ahead = (row > gs) | ((row == gs) & (g_i > g))
        g_rank = g_rank + jnp.where(ahead, 1.0, 0.0)
    g_keep = (g_rank < TOPK_GROUP).astype(F32).reshape(N_GROUPS, 1, tr)
    keep = jnp.broadcast_to(g_keep, (N_GROUPS, GROUP_SIZE, tr)).reshape(e_n, tr) > 0.5
    cm = jnp.where(keep, choice, -jnp.inf)

    e_i = lax.broadcasted_iota(jnp.int32, cm.shape, 0)
    e_rank = jnp.zeros(cm.shape, F32)
    for e in range(e_n):
        row = cm[e:e + 1, :]
        ahead = (row > cm) | ((row == cm) & (e_i > e))
        e_rank = e_rank + jnp.where(ahead, 1.0, 0.0)
    sel = e_rank < TOP_K
    sel_f = sel.astype(F32)
    w = jnp.where(sel, scores, 0.0)
    gate = w / (jnp.sum(w, axis=0, keepdims=True) + 1e-20) * ROUTED_SCALE

    sel_b = sel_f.astype(BF16)
    t_r = lax.broadcasted_iota(jnp.int32, (tr, tr), 0)
    t_c = lax.broadcasted_iota(jnp.int32, (tr, tr), 1)
    before = jnp.dot(sel_b, (t_r < t_c).astype(BF16), preferred_element_type=F32)
    tok_rank = carry_ref[:, 0:1] + before
    carry_ref[...] = carry_ref[...] + jnp.sum(sel_f, axis=1, keepdims=True)
    cnt_ref[...] = carry_ref[...]

    x_r = lax.broadcasted_iota(jnp.int32, (e_n, e_n), 0)
    x_c = lax.broadcasted_iota(jnp.int32, (e_n, e_n), 1)
    slot = jnp.dot((x_c < x_r).astype(BF16), sel_b, preferred_element_type=F32)
    e_f = e_i.astype(F32)
    idx_rows, gate_rows, rank_rows = [], [], []
    for j in range(TOP_K):
        pick = jnp.where(sel & (slot == j), 1.0, 0.0)
        idx_rows.append(jnp.sum(pick * e_f, axis=0, keepdims=True))
        gate_rows.append(jnp.sum(pick * gate, axis=0, keepdims=True))
        rank_rows.append(jnp.sum(pick * tok_rank, axis=0, keepdims=True))
    idx_ref[...] = jnp.concatenate(idx_rows, axis=0).astype(jnp.int32)
    rank_ref[...] = jnp.concatenate(rank_rows, axis=0).astype(jnp.int32)
    gates = jnp.concatenate(gate_rows, axis=0)
    padded = jnp.concatenate([gates, jnp.zeros((LANES - TOP_K, tr), F32)], axis=0)
    gate_tm_ref[...] = padded.T


def _router(x2, wt, bias, tr):
    t = x2.shape[0]
    col = lambda i: (0, i)
    return pl.pallas_call(
        functools.partial(_router_kernel, tr=tr),
        grid=(t // tr,),
        in_specs=[pl.BlockSpec((tr, D_MODEL), lambda i: (i, 0)),
                  pl.BlockSpec((N_EXPERTS, D_MODEL), lambda i: (0, 0)),
                  pl.BlockSpec((N_EXPERTS, 1), lambda i: (0, 0))],
        out_specs=[pl.BlockSpec((TOP_K, tr), col),
                   pl.BlockSpec((TOP_K, tr), col), pl.BlockSpec((tr, LANES), lambda i: (i, 0)),
                   pl.BlockSpec((N_EXPERTS, LANES), lambda i: (0, 0))],
        out_shape=[jax.ShapeDtypeStruct((TOP_K, t), jnp.int32),
                   jax.ShapeDtypeStruct((TOP_K, t), jnp.int32),
                   jax.ShapeDtypeStruct((t, LANES), F32),
                   jax.ShapeDtypeStruct((N_EXPERTS, LANES), F32)],
        scratch_shapes=[pltpu.VMEM((N_EXPERTS, LANES), F32)],
        compiler_params=_cparams(("arbitrary",)),
        name="moe_router",
    )(x2, wt, bias)


ISSUE_UNROLL = 8


def _pack_rows(x):
    lo = pltpu.bitcast(x[:, :X_WORDS].astype(BF16).astype(F32), jnp.uint32)
    hi = pltpu.bitcast(x[:, X_WORDS:].astype(BF16).astype(F32), jnp.uint32)
    return (lo >> 16) | (hi & jnp.uint32(0xFFFF0000))


def _unpack_words(words):
    return (pltpu.bitcast(words << 16, F32), pltpu.bitcast(words & jnp.uint32(0xFFFF0000), F32))


def _slots_kernel(start_ref, idx_ref, rank_ref, dest_ref):
    idx = idx_ref[...]
    dest = rank_ref[...]
    for e in range(N_EXPERTS):
        dest = dest + jnp.where(idx == e, start_ref[e], 0)
    dest_ref[...] = dest


def _slots(pad_start, idx, rank, tl):
    t = idx.shape[1]
    col = pl.BlockSpec((TOP_K, tl), lambda i: (0, i))
    return pl.pallas_call(
        _slots_kernel,
        grid=(t // tl,),
        in_specs=[pl.BlockSpec(memory_space=pltpu.SMEM), col, col],
        out_specs=col,
        out_shape=jax.ShapeDtypeStruct(idx.shape, jnp.int32),
        compiler_params=_cparams(("arbitrary",)),
        name="moe_slots",
    )(pad_start, idx, rank)


def _dispatch_kernel(dest_ref, x_ref, zeros_ref, xs_ref, xp_ref, sem, *, td):
    del zeros_ref
    words = _pack_rows(x_ref[...])
    for j in range(X_SLAB):
        xp_ref[:, j, :] = words[:, j * LANES:(j + 1) * LANES]

    def issue(g, carry):
        for u in range(ISSUE_UNROLL):
            t = g * ISSUE_UNROLL + u
            for j in range(TOP_K):
                pltpu.make_async_copy(xp_ref.at[t], xs_ref.at[dest_ref[j, t]], sem).start(priority=j % 2)
        return carry

    lax.fori_loop(0, td // ISSUE_UNROLL, issue, 0)
    for j in range(TOP_K):
        pltpu.make_async_copy(xp_ref, xs_ref.at[pl.ds(0, td)], sem).wait()


def _dispatch(dest, x2, zeros, td):
    t = x2.shape[0]
    return pl.pallas_call(
        functools.partial(_dispatch_kernel, td=td),
        grid=(t // td,),
        in_specs=[pl.BlockSpec((TOP_K, td), lambda i: (0, i), memory_space=pltpu.SMEM),
                  pl.BlockSpec((td, D_MODEL), lambda i: (i, 0)),
                  pl.BlockSpec(memory_space=pl.ANY)],
        out_specs=pl.BlockSpec(memory_space=pl.ANY),
        out_shape=jax.ShapeDtypeStruct(zeros.shape, zeros.dtype),
        input_output_aliases={2: 0},
        scratch_shapes=[pltpu.VMEM((td, X_SLAB, LANES), jnp.uint32), pltpu.SemaphoreType.DMA],
        compiler_params=_cparams(("arbitrary",)),
        name="moe_dispatch",
    )(dest, x2, zeros)


def _unpack_rows(words):
    return jnp.concatenate(_unpack_words(words), axis=1).astype(BF16)


def _swiglu(xb, w_gu, w_d):
    h = jnp.dot(xb, w_gu, preferred_element_type=F32)
    act = h[:, :EXPERT_FF] * jax.nn.sigmoid(h[:, :EXPERT_FF]) * h[:, EXPERT_FF:]
    return jnp.dot(act.astype(BF16), w_d, preferred_element_type=F32)


def _expert_kernel(blk_ref, exp_ref, used_ref, xs_ref, wgu_ref, wd_ref, ys_ref, *, te):
    del blk_ref, exp_ref
    live = pl.program_id(0) < used_ref[0]

    @pl.when(live)
    def _():
        words = jnp.concatenate(
            [xs_ref[pl.ds(j, te, stride=X_SLAB), :] for j in range(X_SLAB)], axis=1)
        y = _pack_rows(_swiglu(_unpack_rows(words), wgu_ref[0], wd_ref[0]))
        for c in range(X_SLAB):
            ys_ref[pl.ds(c, te, stride=X_SLAB), :] = y[:, c * LANES:(c + 1) * LANES]

    @pl.when(jnp.logical_not(live))
    def _():
        ys_ref[...] = jnp.zeros(ys_ref.shape, ys_ref.dtype)


def _experts(blk, exp, used, xs2, w_gu, w_d, te):
    n_blocks = blk.shape[0]
    n_rows = xs2.shape[0] // X_SLAB
    return pl.pallas_call(
        functools.partial(_expert_kernel, te=te),
        grid_spec=pltpu.PrefetchScalarGridSpec(
            num_scalar_prefetch=3,
            grid=(n_blocks,),
            in_specs=[pl.BlockSpec((te * X_SLAB, LANES), lambda b, blk, exp, used: (blk[b], 0)),
                      pl.BlockSpec((1, D_MODEL, 2 * EXPERT_FF), lambda b, blk, exp, used: (exp[b], 0, 0)),
                      pl.BlockSpec((1, EXPERT_FF, D_MODEL), lambda b, blk, exp, used: (exp[b], 0, 0))],
            out_specs=pl.BlockSpec((te * X_SLAB, LANES), lambda b, blk, exp, used: (b, 0)),
        ),
        out_shape=jax.ShapeDtypeStruct((n_rows * X_SLAB, LANES), jnp.uint32),
        compiler_params=_cparams(("arbitrary",)),
        name="moe_experts",
    )(blk, exp, used, xs2, w_gu, w_d)


def _combine_kernel(dest_ref, dest_next_ref, gate_ref, x_ref, wgu_ref, wd_ref, g_ref, b_ref, ys_ref, ys2_ref,
                    o_ref, ybuf_ref, sem, *, tc):
    i = pl.program_id(0)
    last = pl.num_programs(0) - 1
    slot = i % 2
    rows = TOP_K * tc

    def row_copy(d_ref, j, t, into):
        dst = ybuf_ref.at[pl.ds((into * rows + j * tc + t) * X_SLAB, X_SLAB)]
        return pltpu.make_async_copy(ys_ref.at[d_ref[j, t]], dst, sem.at[into])

    def wait_tile(into):
        pltpu.make_async_copy(ys2_ref.at[pl.ds(0, rows * X_SLAB)],
                              ybuf_ref.at[pl.ds(into * rows * X_SLAB, rows * X_SLAB)], sem.at[into]).wait()

    @pl.when(i == 0)
    def _():
        def issue(t, carry):
            for j in range(TOP_K):
                row_copy(dest_ref, j, t, 0).start(priority=j % 2)
            return carry
        lax.fori_loop(0, tc, issue, 0)

    wait_tile(slot)
    x = x_ref[...]
    gate = gate_ref[...]
    acc_lo = [None] * X_SLAB
    acc_hi = [None] * X_SLAB
    per_phase = tc // TOP_K
    for j in range(TOP_K):
        for t in range(j * per_phase, (j + 1) * per_phase):
            for jj in range(TOP_K):
                row_copy(dest_next_ref, jj, t, 1 - slot).start(priority=jj % 2)
        g_j = gate[:, j:j + 1]
        for c in range(X_SLAB):
            start = (slot * rows + j * tc) * X_SLAB + c
            lo, hi = _unpack_words(ybuf_ref[pl.ds(start, tc, stride=X_SLAB), :])
            acc_lo[c] = g_j * lo if j == 0 else acc_lo[c] + g_j * lo
            acc_hi[c] = g_j * hi if j == 0 else acc_hi[c] + g_j * hi
    routed = jnp.concatenate(acc_lo + acc_hi, axis=1)
    acc = ALPHA * x + routed + _swiglu(x.astype(BF16), wgu_ref[...], wd_ref[...])
    o_ref[...] = _layer_norm(acc, g_ref[...], b_ref[...])

    @pl.when(i == last)
    def _():
        wait_tile(1 - slot)


def _combine(dest, gate_tm, x2, ys, ws_gu, ws_d, g, b, tc):
    t = x2.shape[0]
    n = t // tc
    fixed = lambda i: (0, 0)
    row = lambda i: (i, 0)
    smem = lambda f: pl.BlockSpec((TOP_K, tc), f, memory_space=pltpu.SMEM)
    return pl.pallas_call(
        functools.partial(_combine_kernel, tc=tc),
        grid=(n,),
        in_specs=[smem(lambda i: (0, i)), smem(lambda i: (0, jnp.minimum(i + 1, n - 1))),
                  pl.BlockSpec((tc, LANES), row), pl.BlockSpec((tc, D_MODEL), row),
                  pl.BlockSpec((D_MODEL, 2 * EXPERT_FF), fixed), pl.BlockSpec((EXPERT_FF, D_MODEL), fixed),
                  pl.BlockSpec((1, D_MODEL), fixed), pl.BlockSpec((1, D_MODEL), fixed),
                  pl.BlockSpec(memory_space=pl.ANY), pl.BlockSpec(memory_space=pl.ANY)],
        out_specs=pl.BlockSpec((tc, D_MODEL), row),
        out_shape=jax.ShapeDtypeStruct((t, D_MODEL), F32),
        scratch_shapes=[pltpu.VMEM((2 * TOP_K * tc * X_SLAB, LANES), jnp.uint32),
                        pltpu.SemaphoreType.DMA((2,))],
        compiler_params=_cparams(("arbitrary",)),
        name="moe_combine",
    )(dest, dest, gate_tm, x2, ws_gu, ws_d, g, b, ys.reshape(-1, X_SLAB, LANES), ys)


def _moe_layer(x2, w_router, bias, w_gu, w_d, ws_gu, ws_d, ln_g, ln_b, te=512, tr=512, td=256, tc=128):
    t = x2.shape[0]
    n_blocks = t * TOP_K // te + N_EXPERTS
    n_rows = n_blocks * te
    idx, rank, gate_tm, cnt = _router(x2, w_router.T.astype(F32), bias.reshape(-1, 1).astype(F32), tr)

    counts = cnt[:, 0].astype(jnp.int32)
    padded = (counts + te - 1) // te * te
    pad_end = jnp.cumsum(padded)
    pad_start = pad_end - padded
    dest = _slots(pad_start, idx, rank, min(t, 4096))
    used = pad_end[-1] // te
    blk = jnp.minimum(jnp.arange(n_blocks, dtype=jnp.int32), used - 1)
    exp = jnp.sum((pad_end[None, :] // te <= blk[:, None]).astype(jnp.int32), axis=1)
    exp = jnp.minimum(exp, N_EXPERTS - 1)

    xs = _dispatch(dest, x2, jnp.zeros((n_rows, X_SLAB, LANES), jnp.uint32), td)
    ys = _experts(blk, exp.astype(jnp.int32), used.reshape(1), xs.reshape(n_rows * X_SLAB, LANES),
                  w_gu.astype(BF16), w_d.astype(BF16), te)
    return _combine(dest, gate_tm, x2, ys, ws_gu.astype(BF16), ws_d.astype(BF16),
                    ln_g.reshape(1, -1), ln_b.reshape(1, -1), tc)


def kernel(x, attn_w_in, attn_w_out, attn_lambda, attn_subln, hgrn_w_in, hgrn_w_out, hgrn_lower_bounds, hgrn_norm, moe_router, moe_router_bias, moe_w_gate_up, moe_w_down, shared_w_gate_up, shared_w_down, ln_gain, ln_bias):
    bsz, seq, _ = x.shape
    tables = _rotary_tables(seq)
    lb = jax.nn.softmax(hgrn_lower_bounds.astype(F32), axis=0)
    lb = jnp.cumsum(lb, axis=0) - lb[0]
    x2 = x.reshape(bsz * seq, D_MODEL)
    for layer in range(DEPTH):
        j = layer // 2
        if layer % 2 == 0:
            lambda_init = 0.8 - 0.6 * math.exp(-0.3 * layer)
            x2 = _attn_layer(x2, bsz, seq, attn_w_in[j], attn_w_out[j], attn_lambda[j], attn_subln[j],
                             lambda_init, ln_gain[layer, 0], ln_bias[layer, 0], tables)
        else:
            x2 = _hgrn_layer(x2, bsz, seq, hgrn_w_in[j], hgrn_w_out[j], lb[layer], hgrn_norm[j],
                             ln_gain[layer, 0], ln_bias[layer, 0])
        x2 = _moe_layer(x2, moe_router[layer], moe_router_bias[layer], moe_w_gate_up[layer],
                        moe_w_down[layer], shared_w_gate_up[layer], shared_w_down[layer],
                        ln_gain[layer, 1], ln_bias[layer, 1])
    return x2.reshape(bsz, seq, D_MODEL)
```

```python
import functools
import math

import jax
import jax.numpy as jnp
from jax import lax
from jax.experimental import pallas as pl
from jax.experimental.pallas import tpu as pltpu

F32 = jnp.float32
BF16 = jnp.bfloat16

D_MODEL = 1024
DEPTH = 4
CHUNK = 64
A_HEADS = 8
A_HEAD_DIM = 64
ROT_DIM = 16
ROPE_THETA = 500000.0
H_HEADS = 8
H_EXPAND = 128
N_EXPERTS = 64
N_GROUPS = 8
TOPK_GROUP = 4
TOP_K = 8
EXPERT_FF = 256
ROUTED_SCALE = 2.5
ALPHA = (2 * DEPTH) ** 0.25
LN_EPS = 1e-5
LANES = 128
VMEM_LIMIT = 48 * 1024 * 1024


def _cparams(sem):
    return pltpu.CompilerParams(dimension_semantics=sem, vmem_limit_bytes=VMEM_LIMIT)


def _layer_norm(y, g, b):
    mu = jnp.mean(y, axis=-1, keepdims=True)
    yc = y - mu
    var = jnp.mean(yc * yc, axis=-1, keepdims=True)
    return yc * lax.rsqrt(var + LN_EPS) * g + b


def _dot_nt(a, b):
    return lax.dot_general(a, b, (((1,), (1,)), ((), ())), preferred_element_type=F32)


def _dot_tn(a, b):
    return lax.dot_general(a, b, (((0,), (0,)), ((), ())), preferred_element_type=F32)


def _attn_proj_kernel(x_ref, w_ref, cf_ref, s1_ref, s2_ref, q_ref, k_ref, v_ref):
    xb = x_ref[...].astype(BF16)
    cf, s1, s2 = cf_ref[...], s1_ref[...], s2_ref[...]

    def rotary(y):
        outs = []
        for c in range(D_MODEL // LANES):
            yc = y[:, c * LANES:(c + 1) * LANES]
            outs.append(yc * cf + pltpu.roll(yc, LANES - ROT_DIM // 2, 1) * s1
                        + pltpu.roll(yc, ROT_DIM // 2, 1) * s2)
        return jnp.concatenate(outs, axis=1)

    q = jnp.dot(xb, w_ref[:, 0:D_MODEL], preferred_element_type=F32)
    q_ref[...] = (rotary(q) * (A_HEAD_DIM ** -0.5 * math.log2(math.e))).astype(BF16)
    k = jnp.dot(xb, w_ref[:, D_MODEL:2 * D_MODEL], preferred_element_type=F32)
    k_ref[...] = rotary(k).astype(BF16)
    v = jnp.dot(xb, w_ref[:, 2 * D_MODEL:3 * D_MODEL], preferred_element_type=F32)
    v_ref[...] = v.astype(BF16)


def _attn_proj(x2, w, cf, s1, s2, seq, tm):
    t = x2.shape[0]
    nseq = seq // tm
    row = lambda i: (i, 0)
    tab = lambda i: (i % nseq, 0)
    out = jax.ShapeDtypeStruct((t, D_MODEL), BF16)
    return pl.pallas_call(
        _attn_proj_kernel,
        grid=(t // tm,),
        in_specs=[pl.BlockSpec((tm, D_MODEL), row),
                  pl.BlockSpec((D_MODEL, 3 * D_MODEL), lambda i: (0, 0)),
                  pl.BlockSpec((tm, LANES), tab), pl.BlockSpec((tm, LANES), tab),
                  pl.BlockSpec((tm, LANES), tab)],
        out_specs=[pl.BlockSpec((tm, D_MODEL), row)] * 3,
        out_shape=[out, out, out],
        compiler_params=_cparams(("arbitrary",)),
        name="attn_proj",
    )(x2, w, cf, s1, s2)


ATTN_UNROLL = 4


def _attn_kernel(lam_ref, q_ref, k_ref, v_ref, g_ref, o_ref, m1_ref, a1_ref, m2_ref, a2_ref,
                 *, tq, out_scale):
    i = pl.program_id(2)
    q = q_ref[0]
    lane = lax.broadcasted_iota(jnp.int32, q.shape, 1)
    zero = jnp.zeros_like(q)
    q1 = jnp.where(lane < A_HEAD_DIM, q, zero)
    q2 = jnp.where(lane >= A_HEAD_DIM, q, zero)
    q12 = jnp.concatenate([q1, q2], axis=0)
    ones = jnp.ones((tq, LANES), BF16)

    for m_ref, a_ref in ((m1_ref, a1_ref), (m2_ref, a2_ref)):
        m_ref[...] = jnp.full(m_ref.shape, -jnp.inf, F32)
        a_ref[...] = jnp.zeros(a_ref.shape, F32)

    def update(s, vb1, m_ref, a_ref):
        tiles = [s[:, c * LANES:(c + 1) * LANES] for c in range(s.shape[1] // LANES)]
        mc = functools.reduce(jnp.maximum, tiles)
        m_old = m_ref[...]
        m_new = jnp.maximum(m_old, jnp.max(mc, axis=-1, keepdims=True))
        alpha = jnp.exp2(m_old - m_new)
        pb = jnp.concatenate([jnp.exp2((t - m_new).astype(BF16)) for t in tiles], axis=1)
        pv = jnp.dot(pb, vb1, preferred_element_type=F32)
        a_ref[:, :LANES] = alpha * a_ref[:, :LANES] + pv[:, :LANES]
        a_ref[:, LANES:] = alpha * a_ref[:, LANES:] + pv[:, LANES:]
        m_ref[...] = m_new

    def block(j, mask):
        start = pl.multiple_of(j * tq, tq)
        kb = k_ref[0, pl.ds(start, tq), :]
        vb1 = jnp.concatenate([v_ref[0, pl.ds(start, tq), :], ones], axis=1)
        s = _dot_nt(q12, kb)
        s1, s2 = s[:tq], s[tq:]
        if mask is not None:
            s1 = jnp.where(mask, s1, -jnp.inf)
            s2 = jnp.where(mask, s2, -jnp.inf)
        update(s1, vb1, m1_ref, a1_ref)
        update(s2, vb1, m2_ref, a2_ref)

    def many_blocks(p, carry):
        for u in range(ATTN_UNROLL):
            block(ATTN_UNROLL * p + u, None)
        return carry

    def one_block(j, carry):
        block(j, None)
        return carry

    lax.fori_loop(0, i // ATTN_UNROLL, many_blocks, 0)
    lax.fori_loop(i // ATTN_UNROLL * ATTN_UNROLL, i, one_block, 0)

    rq = lax.broadcasted_iota(jnp.int32, (tq, tq), 0) // CHUNK
    ck = lax.broadcasted_iota(jnp.int32, (tq, tq), 1) // CHUNK
    block(i, ck <= rq)


    lam = lam_ref[0]
    o = a1_ref[:, :LANES] / a1_ref[:, LANES:] - lam * (a2_ref[:, :LANES] / a2_ref[:, LANES:])
    o = o * lax.rsqrt(jnp.mean(o * o, axis=-1, keepdims=True) + LN_EPS) * g_ref[...]
    o_ref[0] = (o * out_scale).astype(o_ref.dtype)


def _attention(lam, q, k, v, g, lambda_init, tq):
    b, s, _ = q.shape
    kern = functools.partial(_attn_kernel, tq=tq, out_scale=1.0 - lambda_init)
    kv_spec = pl.BlockSpec((1, s, 2 * A_HEAD_DIM), lambda bi, h, i: (bi, 0, h))
    qo_spec = pl.BlockSpec((1, tq, 2 * A_HEAD_DIM), lambda bi, h, i: (bi, i, h))
    vec = lambda n: pltpu.VMEM((tq, n), F32)
    return pl.pallas_call(
        kern,
        grid=(b, A_HEADS, s // tq),
        in_specs=[pl.BlockSpec(memory_space=pltpu.SMEM), qo_spec, kv_spec, kv_spec,
                  pl.BlockSpec((1, 2 * A_HEAD_DIM), lambda bi, h, i: (0, 0))],
        out_specs=qo_spec,
        out_shape=jax.ShapeDtypeStruct((b, s, D_MODEL), BF16),
        scratch_shapes=[vec(LANES), vec(2 * LANES)] * 2,
        compiler_params=_cparams(("arbitrary", "arbitrary", "arbitrary")),
        name="diff_attention",
    )(lam, q, k, v, g)


def _out_proj_kernel(o_ref, x_ref, w_ref, g_ref, b_ref, y_ref):
    h = jnp.dot(o_ref[...], w_ref[...], preferred_element_type=F32)
    y_ref[...] = _layer_norm(ALPHA * x_ref[...] + h, g_ref[...], b_ref[...])


def _out_proj(o2, x2, w, g, b, tm):
    t = x2.shape[0]
    row = lambda i: (i, 0)
    fixed = lambda i: (0, 0)
    return pl.pallas_call(
        _out_proj_kernel,
        grid=(t // tm,),
        in_specs=[pl.BlockSpec((tm, D_MODEL), row), pl.BlockSpec((tm, D_MODEL), row),
                  pl.BlockSpec((D_MODEL, D_MODEL), fixed),
                  pl.BlockSpec((1, D_MODEL), fixed), pl.BlockSpec((1, D_MODEL), fixed)],
        out_specs=pl.BlockSpec((tm, D_MODEL), row),
        out_shape=jax.ShapeDtypeStruct((t, D_MODEL), F32),
        compiler_params=_cparams(("arbitrary",)),
        name="out_proj_ln",
    )(o2, x2, w, g, b)


def _rotary_tables(seq):
    pos = jnp.arange(seq, dtype=F32)
    inv_freq = ROPE_THETA ** (-jnp.arange(0, ROT_DIM, 2, dtype=F32) / ROT_DIM)
    ang = pos[:, None] * inv_freq[None, :]
    cos, sin = jnp.cos(ang), jnp.sin(ang)
    half = ROT_DIM // 2
    pad = A_HEAD_DIM - ROT_DIM
    one = jnp.ones((seq, pad), F32)
    zero = jnp.zeros((seq, pad), F32)
    zh = jnp.zeros((seq, half), F32)
    cf = jnp.concatenate([cos, cos, one], axis=1)
    s1 = jnp.concatenate([-sin, zh, zero], axis=1)
    s2 = jnp.concatenate([zh, sin, zero], axis=1)
    rep = LANES // A_HEAD_DIM
    return tuple(jnp.tile(t, (1, rep)) for t in (cf, s1, s2))


def _attn_layer(x2, bsz, seq, w_in, w_out, lam_params, subln, lambda_init, ln_g, ln_b, tables,
                tm=512, tq=512):
    q, k, v = _attn_proj(x2, w_in.astype(BF16), *tables, seq, tm)
    lp = lam_params.astype(F32)
    lam = jnp.exp(jnp.sum(lp[0] * lp[1])) - jnp.exp(jnp.sum(lp[2] * lp[3])) + lambda_init
    shp = (bsz, seq, D_MODEL)
    o = _attention(lam.reshape(1), q.reshape(shp), k.reshape(shp), v.reshape(shp),
                   subln.reshape(1, -1).astype(F32), lambda_init, tq)
    return _out_proj(o.reshape(bsz * seq, D_MODEL), x2, w_out.astype(BF16),
                     ln_g.reshape(1, -1), ln_b.reshape(1, -1), tm)


H_CHUNK = 128
H_SUB = 16


def _hgrn_proj_kernel(x_ref, w_ref, lb_ref, q_ref, k_ref, v_ref, lf_ref, g_ref):
    xb = x_ref[...].astype(BF16)
    w = H_HEADS * H_EXPAND
    q = jnp.dot(xb, w_ref[:, 0:w], preferred_element_type=F32)
    q_ref[...] = q * jax.nn.sigmoid(q)
    f = jnp.dot(xb, w_ref[:, w:2 * w], preferred_element_type=F32)
    lb = lb_ref[...]
    forget = lb + (1.0 - lb) * jax.nn.sigmoid(f)
    lf_ref[...] = jnp.log(forget)
    k_ref[...] = 1.0 - forget
    v_ref[...] = jnp.dot(xb, w_ref[:, 2 * w:3 * w], preferred_element_type=F32)
    g = jnp.dot(xb, w_ref[:, 3 * w:4 * w], preferred_element_type=F32)
    g_ref[...] = g * jax.nn.sigmoid(g)


def _hgrn_proj(x2, w, lb, tm):
    t = x2.shape[0]
    row = lambda i: (i, 0)
    fixed = lambda i: (0, 0)
    out = jax.ShapeDtypeStruct((t, D_MODEL), F32)
    return pl.pallas_call(
        _hgrn_proj_kernel,
        grid=(t // tm,),
        in_specs=[pl.BlockSpec((tm, D_MODEL), row), pl.BlockSpec((D_MODEL, 4 * D_MODEL), fixed),
                  pl.BlockSpec((1, D_MODEL), fixed)],
        out_specs=[pl.BlockSpec((tm, D_MODEL), row)] * 5,
        out_shape=[out] * 5,
        compiler_params=_cparams(("arbitrary",)),
        name="hgrn_proj",
    )(x2, w, lb)


def _hgrn_chunk(q, k, v, lf, st):
    c, nb = H_CHUNK, H_CHUNK // H_SUB
    r_i = lax.broadcasted_iota(jnp.int32, (c, c), 0)
    c_i = lax.broadcasted_iota(jnp.int32, (c, c), 1)
    tri = (r_i >= c_i).astype(BF16)
    hi = lf.astype(BF16)
    lo = (lf - hi.astype(F32)).astype(BF16)
    b2 = jnp.dot(tri, jnp.concatenate([hi, lo], axis=1), preferred_element_type=F32)
    b = (b2[:, :H_EXPAND] + b2[:, H_EXPAND:]) * math.log2(math.e)
    b_last = b[c - 1:c, :]

    o = _dot_nt((q * jnp.exp2(b)).astype(BF16), st.astype(BF16))
    ke = k * jnp.exp2(b_last - b)
    st_new = st * jnp.exp2(b_last) + _dot_tn(v.astype(BF16), ke.astype(BF16))

    ends = [b[(j + 1) * H_SUB - 1:(j + 1) * H_SUB, :] for j in range(nb)]
    e_blk = jnp.concatenate([jnp.broadcast_to(e, (H_SUB, H_EXPAND)) for e in ends], axis=0)
    kt = k * jnp.exp2(e_blk - b)
    row_blk = lax.broadcasted_iota(jnp.int32, (c, H_EXPAND), 0) // H_SUB
    q_big = jnp.concatenate(
        [(q * jnp.exp2(jnp.minimum(b - ends[j], 0.0))).astype(BF16) for j in range(nb)], axis=1)
    k_big = jnp.concatenate(
        [jnp.where(row_blk == j, kt, 0.0).astype(BF16) for j in range(nb)], axis=1)
    a = _dot_nt(q_big, k_big)
    a = jnp.where(r_i // H_SUB > c_i // H_SUB, a, 0.0)
    o = o + jnp.dot(a.astype(BF16), v.astype(BF16), preferred_element_type=F32)

    t_idx = lax.broadcasted_iota(jnp.int32, (H_SUB, 1), 0)
    diag = []
    for j in range(nb):
        sl = slice(j * H_SUB, (j + 1) * H_SUB)
        qb, kb, vb, bb = q[sl], k[sl], v[sl], b[sl]
        od = jnp.zeros((H_SUB, H_EXPAND), F32)
        for s in range(H_SUB):
            e = jnp.exp2(bb - bb[s:s + 1, :])
            col = jnp.sum(qb * (kb[s:s + 1, :] * e), axis=-1, keepdims=True)
            od = od + jnp.where(t_idx >= s, col, 0.0) * vb[s:s + 1, :]
        diag.append(od)
    return o + jnp.concatenate(diag, axis=0), st_new


def _hgrn_kernel(q_ref, k_ref, v_ref, lf_ref, g_ref, ng_ref, o_ref, st_ref, *, n_chunks):
    @pl.when(pl.program_id(2) == 0)
    def _():
        st_ref[...] = jnp.zeros(st_ref.shape, F32)

    st = st_ref[...]
    for ci in range(n_chunks):
        sl = slice(ci * H_CHUNK, (ci + 1) * H_CHUNK)
        o, st = _hgrn_chunk(q_ref[sl, :], k_ref[sl, :], v_ref[sl, :], lf_ref[sl, :], st)
        o = o * lax.rsqrt(jnp.mean(o * o, axis=-1, keepdims=True) + LN_EPS) * ng_ref[...]
        o_ref[sl, :] = (o * g_ref[sl, :]).astype(o_ref.dtype)
    st_ref[...] = st


def _hgrn_recurrence(q, k, v, lf, g, ng, bsz, seq, tc):
    t = q.shape[0]
    nt = seq // tc
    blk = pl.BlockSpec((tc, H_EXPAND), lambda b, h, c: (b * nt + c, h))
    return pl.pallas_call(
        functools.partial(_hgrn_kernel, n_chunks=tc // H_CHUNK),
        grid=(bsz, H_HEADS, nt),
        in_specs=[blk] * 5 + [pl.BlockSpec((1, H_EXPAND), lambda b, h, c: (0, 0))],
        out_specs=blk,
        out_shape=jax.ShapeDtypeStruct((t, D_MODEL), BF16),
        scratch_shapes=[pltpu.VMEM((H_EXPAND, H_EXPAND), F32)],
        compiler_params=_cparams(("arbitrary", "arbitrary", "arbitrary")),
        name="hgrn_recurrence",
    )(q, k, v, lf, g, ng)


def _hgrn_layer(x2, bsz, seq, w_in, w_out, lb, norm_g, ln_g, ln_b, tm=256, tc=512):
    q, k, v, lf, g = _hgrn_proj(x2, w_in.astype(BF16), lb.reshape(1, -1).astype(F32), tm)
    o = _hgrn_recurrence(q, k, v, lf, g, norm_g.reshape(1, -1).astype(F32), bsz, seq, tc)
    return _out_proj(o, x2, w_out.astype(BF16), ln_g.reshape(1, -1), ln_b.reshape(1, -1), 2 * tm)


GROUP_SIZE = N_EXPERTS // N_GROUPS
X_WORDS = D_MODEL // 2
X_SLAB = X_WORDS // LANES


def _router_kernel(x_ref, wt_ref, bias_ref, idx_ref, rank_ref, gate_tm_ref, cnt_ref,
                   carry_ref, *, tr):
    @pl.when(pl.program_id(0) == 0)
    def _():
        carry_ref[...] = jnp.zeros(carry_ref.shape, F32)

    e_n = N_EXPERTS
    logits = lax.dot_general(wt_ref[...], x_ref[...], (((1,), (1,)), ((), ())),
                             precision=lax.Precision.HIGHEST, preferred_element_type=F32)
    scores = jax.nn.sigmoid(logits)
    choice = scores + bias_ref[...]

    ch3 = choice.reshape(N_GROUPS, GROUP_SIZE, tr)
    sub = lax.broadcasted_iota(jnp.int32, ch3.shape, 1)
    m1 = jnp.max(ch3, axis=1, keepdims=True)
    first = jnp.min(jnp.where(ch3 == m1, sub, GROUP_SIZE), axis=1, keepdims=True)
    m2 = jnp.max(jnp.where(sub == first, -jnp.inf, ch3), axis=1, keepdims=True)
    gs = (m1 + m2).reshape(N_GROUPS, tr)

    g_i = lax.broadcasted_iota(jnp.int32, gs.shape, 0)
    g_rank = jnp.zeros(gs.shape, F32)
    for g in range(N_GROUPS):
        row = gs[g:g + 1, :]
        ahead = (row > gs) | ((row == gs) & (g_i > g))
        g_rank = g_rank + jnp.where(ahead, 1.0, 0.0)
    g_keep = (g_rank < TOPK_GROUP).astype(F32).reshape(N_GROUPS, 1, tr)
    keep = jnp.broadcast_to(g_keep, (N_GROUPS, GROUP_SIZE, tr)).reshape(e_n, tr) > 0.5
    cm = jnp.where(keep, choice, -jnp.inf)

    e_i = lax.broadcasted_iota(jnp.int32, cm.shape, 0)
    sub_i = lax.broadcasted_iota(jnp.int32, (GROUP_SIZE, tr), 0)
    cm_g = [cm[g * GROUP_SIZE:(g + 1) * GROUP_SIZE, :] for g in range(N_GROUPS)]
    ranks = [jnp.zeros((GROUP_SIZE, tr), F32) for _ in range(N_GROUPS)]
    for e in range(e_n):
        ge, se = divmod(e, GROUP_SIZE)
        row = jnp.broadcast_to(cm_g[ge][se:se + 1, :], (GROUP_SIZE, tr))
        for g in range(N_GROUPS):
            if g > ge:
                ahead = row >= cm_g[g]
            elif g < ge:
                ahead = row > cm_g[g]
            else:
                ahead = (row > cm_g[g]) | ((row == cm_g[g]) & (sub_i > se))
            ranks[g] = ranks[g] + jnp.where(ahead, 1.0, 0.0)
    sel = jnp.concatenate(ranks, axis=0) < TOP_K
    sel_f = sel.astype(F32)
    w = jnp.where(sel, scores, 0.0)
    gate = w / (jnp.sum(w, axis=0, keepdims=True) + 1e-20) * ROUTED_SCALE

    sel_b = sel_f.astype(BF16)
    t_r = lax.broadcasted_iota(jnp.int32, (tr, tr), 0)
    t_c = lax.broadcasted_iota(jnp.int32, (tr, tr), 1)
    before = jnp.dot(sel_b, (t_r < t_c).astype(BF16), preferred_element_type=F32)
    tok_rank = carry_ref[:, 0:1] + before
    carry_ref[...] = carry_ref[...] + jnp.sum(sel_f, axis=1, keepdims=True)
    cnt_ref[...] = carry_ref[...]

    x_r = lax.broadcasted_iota(jnp.int32, (e_n, e_n), 0)
    x_c = lax.broadcasted_iota(jnp.int32, (e_n, e_n), 1)
    slot = jnp.dot((x_c < x_r).astype(BF16), sel_b, preferred_element_type=F32)
    e_f = e_i.astype(F32)
    idx_rows, gate_rows, rank_rows = [], [], []
    for j in range(TOP_K):
        pick = jnp.where(sel & (slot == j), 1.0, 0.0)
        idx_rows.append(jnp.sum(pick * e_f, axis=0, keepdims=True))
        gate_rows.append(jnp.sum(pick * gate, axis=0, keepdims=True))
        rank_rows.append(jnp.sum(pick * tok_rank, axis=0, keepdims=True))
    idx_ref[...] = jnp.concatenate(idx_rows, axis=0).astype(jnp.int32)
    rank_ref[...] = jnp.concatenate(rank_rows, axis=0).astype(jnp.int32)
    gates = jnp.concatenate(gate_rows, axis=0)
    padded = jnp.concatenate([gates, jnp.zeros((LANES - TOP_K, tr), F32)], axis=0)
    gate_tm_ref[...] = padded.T


def _router(x2, wt, bias, tr):
    t = x2.shape[0]
    col = lambda i: (0, i)
    return pl.pallas_call(
        functools.partial(_router_kernel, tr=tr),
        grid=(t // tr,),
        in_specs=[pl.BlockSpec((tr, D_MODEL), lambda i: (i, 0)),
                  pl.BlockSpec((N_EXPERTS, D_MODEL), lambda i: (0, 0)),
                  pl.BlockSpec((N_EXPERTS, 1), lambda i: (0, 0))],
        out_specs=[pl.BlockSpec((TOP_K, tr), col),
                   pl.BlockSpec((TOP_K, tr), col), pl.BlockSpec((tr, LANES), lambda i: (i, 0)),
                   pl.BlockSpec((N_EXPERTS, LANES), lambda i: (0, 0))],
        out_shape=[jax.ShapeDtypeStruct((TOP_K, t), jnp.int32),
                   jax.ShapeDtypeStruct((TOP_K, t), jnp.int32),
                   jax.ShapeDtypeStruct((t, LANES), F32),
                   jax.ShapeDtypeStruct((N_EXPERTS, LANES), F32)],
        scratch_shapes=[pltpu.VMEM((N_EXPERTS, LANES), F32)],
        compiler_params=_cparams(("arbitrary",)),
        name="moe_router",
    )(x2, wt, bias)


ISSUE_UNROLL = 8


def _pack_rows(x):
    lo = pltpu.bitcast(x[:, :X_WORDS].astype(BF16).astype(F32), jnp.uint32)
    hi = pltpu.bitcast(x[:, X_WORDS:].astype(BF16).astype(F32), jnp.uint32)
    return (lo >> 16) | (hi & jnp.uint32(0xFFFF0000))


def _unpack_words(words):
    return (pltpu.bitcast(words << 16, F32), pltpu.bitcast(words & jnp.uint32(0xFFFF0000), F32))


def _slots_kernel(start_ref, idx_ref, rank_ref, dest_ref):
    idx = idx_ref[...]
    dest = rank_ref[...]
    for e in range(N_EXPERTS):
        dest = dest + jnp.where(idx == e, start_ref[e], 0)
    dest_ref[...] = dest


def _slots(pad_start, idx, rank, tl):
    t = idx.shape[1]
    col = pl.BlockSpec((TOP_K, tl), lambda i: (0, i))
    return pl.pallas_call(
        _slots_kernel,
        grid=(t // tl,),
        in_specs=[pl.BlockSpec(memory_space=pltpu.SMEM), col, col],
        out_specs=col,
        out_shape=jax.ShapeDtypeStruct(idx.shape, jnp.int32),
        compiler_params=_cparams(("arbitrary",)),
        name="moe_slots",
    )(pad_start, idx, rank)


def _dispatch_kernel(dest_ref, x_ref, zeros_ref, xs_ref, xp_ref, sem, *, td):
    del zeros_ref
    words = _pack_rows(x_ref[...])
    for j in range(X_SLAB):
        xp_ref[:, j, :] = words[:, j * LANES:(j + 1) * LANES]

    def issue(g, carry):
        for u in range(ISSUE_UNROLL):
            t = g * ISSUE_UNROLL + u
            for j in range(TOP_K):
                pltpu.make_async_copy(xp_ref.at[t], xs_ref.at[dest_ref[j, t]], sem).start(priority=j % 2)
        return carry

    lax.fori_loop(0, td // ISSUE_UNROLL, issue, 0)
    for j in range(TOP_K):
        pltpu.make_async_copy(xp_ref, xs_ref.at[pl.ds(0, td)], sem).wait()


def _dispatch(dest, x2, zeros, td):
    t = x2.shape[0]
    return pl.pallas_call(
        functools.partial(_dispatch_kernel, td=td),
        grid=(t // td,),
        in_specs=[pl.BlockSpec((TOP_K, td), lambda i: (0, i), memory_space=pltpu.SMEM),
                  pl.BlockSpec((td, D_MODEL), lambda i: (i, 0)),
                  pl.BlockSpec(memory_space=pl.ANY)],
        out_specs=pl.BlockSpec(memory_space=pl.ANY),
        out_shape=jax.ShapeDtypeStruct(zeros.shape, zeros.dtype),
        input_output_aliases={2: 0},
        scratch_shapes=[pltpu.VMEM((td, X_SLAB, LANES), jnp.uint32), pltpu.SemaphoreType.DMA],
        compiler_params=_cparams(("arbitrary",)),
        name="moe_dispatch",
    )(dest, x2, zeros)


def _unpack_rows(words):
    return jnp.concatenate(_unpack_words(words), axis=1).astype(BF16)


def _swiglu(xb, w_gu, w_d):
    h = jnp.dot(xb, w_gu, preferred_element_type=F32)
    act = h[:, :EXPERT_FF] * jax.nn.sigmoid(h[:, :EXPERT_FF]) * h[:, EXPERT_FF:]
    return jnp.dot(act.astype(BF16), w_d, preferred_element_type=F32)


def _expert_kernel(blk_ref, exp_ref, used_ref, xs_ref, wgu_ref, wd_ref, ys_ref, *, te):
    del blk_ref, exp_ref
    live = pl.program_id(0) < used_ref[0]

    @pl.when(live)
    def _():
        words = jnp.concatenate(
            [xs_ref[pl.ds(j, te, stride=X_SLAB), :] for j in range(X_SLAB)], axis=1)
        y = _pack_rows(_swiglu(_unpack_rows(words), wgu_ref[0], wd_ref[0]))
        for c in range(X_SLAB):
            ys_ref[pl.ds(c, te, stride=X_SLAB), :] = y[:, c * LANES:(c + 1) * LANES]

    @pl.when(jnp.logical_not(live))
    def _():
        ys_ref[...] = jnp.zeros(ys_ref.shape, ys_ref.dtype)


def _experts(blk, exp, used, xs2, w_gu, w_d, te):
    n_blocks = blk.shape[0]
    n_rows = xs2.shape[0] // X_SLAB
    return pl.pallas_call(
        functools.partial(_expert_kernel, te=te),
        grid_spec=pltpu.PrefetchScalarGridSpec(
            num_scalar_prefetch=3,
            grid=(n_blocks,),
            in_specs=[pl.BlockSpec((te * X_SLAB, LANES), lambda b, blk, exp, used: (blk[b], 0)),
                      pl.BlockSpec((1, D_MODEL, 2 * EXPERT_FF), lambda b, blk, exp, used: (exp[b], 0, 0)),
                      pl.BlockSpec((1, EXPERT_FF, D_MODEL), lambda b, blk, exp, used: (exp[b], 0, 0))],
            out_specs=pl.BlockSpec((te * X_SLAB, LANES), lambda b, blk, exp, used: (b, 0)),
        ),
        out_shape=jax.ShapeDtypeStruct((n_rows * X_SLAB, LANES), jnp.uint32),
        compiler_params=_cparams(("arbitrary",)),
        name="moe_experts",
    )(blk, exp, used, xs2, w_gu, w_d)


def _combine_kernel(dest_ref, dest_next_ref, gate_ref, x_ref, wgu_ref, wd_ref, g_ref, b_ref, ys_ref, ys2_ref,
                    o_ref, ybuf_ref, sem, *, tc):
    i = pl.program_id(0)
    last = pl.num_programs(0) - 1
    slot = i % 2
    rows = TOP_K * tc

    def row_copy(d_ref, j, t, into):
        dst = ybuf_ref.at[pl.ds((into * rows + j * tc + t) * X_SLAB, X_SLAB)]
        return pltpu.make_async_copy(ys_ref.at[d_ref[j, t]], dst, sem.at[into])

    def wait_tile(into):
        pltpu.make_async_copy(ys2_ref.at[pl.ds(0, rows * X_SLAB)],
                              ybuf_ref.at[pl.ds(into * rows * X_SLAB, rows * X_SLAB)], sem.at[into]).wait()

    @pl.when(i == 0)
    def _():
        def issue(t, carry):
            for j in range(TOP_K):
                row_copy(dest_ref, j, t, 0).start(priority=j % 2)
            return carry
        lax.fori_loop(0, tc, issue, 0)

    wait_tile(slot)
    x = x_ref[...]
    gate = gate_ref[...]
    acc_lo = [None] * X_SLAB
    acc_hi = [None] * X_SLAB
    per_phase = tc // TOP_K
    for j in range(TOP_K):
        for t in range(j * per_phase, (j + 1) * per_phase):
            for jj in range(TOP_K):
                row_copy(dest_next_ref, jj, t, 1 - slot).start(priority=jj % 2)
        g_j = gate[:, j:j + 1]
        for c in range(X_SLAB):
            start = (slot * rows + j * tc) * X_SLAB + c
            lo, hi = _unpack_words(ybuf_ref[pl.ds(start, tc, stride=X_SLAB), :])
            acc_lo[c] = g_j * lo if j == 0 else acc_lo[c] + g_j * lo
            acc_hi[c] = g_j * hi if j == 0 else acc_hi[c] + g_j * hi
    routed = jnp.concatenate(acc_lo + acc_hi, axis=1)
    acc = ALPHA * x + routed + _swiglu(x.astype(BF16), wgu_ref[...], wd_ref[...])
    o_ref[...] = _layer_norm(acc, g_ref[...], b_ref[...])

    @pl.when(i == last)
    def _():
        wait_tile(1 - slot)


def _combine(dest, gate_tm, x2, ys, ws_gu, ws_d, g, b, tc):
    t = x2.shape[0]
    n = t // tc
    fixed = lambda i: (0, 0)
    row = lambda i: (i, 0)
    smem = lambda f: pl.BlockSpec((TOP_K, tc), f, memory_space=pltpu.SMEM)
    return pl.pallas_call(
        functools.partial(_combine_kernel, tc=tc),
        grid=(n,),
        in_specs=[smem(lambda i: (0, i)), smem(lambda i: (0, jnp.minimum(i + 1, n - 1))),
                  pl.BlockSpec((tc, LANES), row), pl.BlockSpec((tc, D_MODEL), row),
                  pl.BlockSpec((D_MODEL, 2 * EXPERT_FF), fixed), pl.BlockSpec((EXPERT_FF, D_MODEL), fixed),
                  pl.BlockSpec((1, D_MODEL), fixed), pl.BlockSpec((1, D_MODEL), fixed),
                  pl.BlockSpec(memory_space=pl.ANY), pl.BlockSpec(memory_space=pl.ANY)],
        out_specs=pl.BlockSpec((tc, D_MODEL), row),
        out_shape=jax.ShapeDtypeStruct((t, D_MODEL), F32),
        scratch_shapes=[pltpu.VMEM((2 * TOP_K * tc * X_SLAB, LANES), jnp.uint32),
                        pltpu.SemaphoreType.DMA((2,))],
        compiler_params=_cparams(("arbitrary",)),
        name="moe_combine",
    )(dest, dest, gate_tm, x2, ws_gu, ws_d, g, b, ys.reshape(-1, X_SLAB, LANES), ys)


def _moe_layer(x2, w_router, bias, w_gu, w_d, ws_gu, ws_d, ln_g, ln_b, sorted_buf=None,
               te=512, tr=512, td=256, tc=128):
    t = x2.shape[0]
    n_blocks = t * TOP_K // te + N_EXPERTS
    n_rows = n_blocks * te
    if sorted_buf is None:
        sorted_buf = jnp.zeros((n_rows, X_SLAB, LANES), jnp.uint32)
    idx, rank, gate_tm, cnt = _router(x2, w_router.T.astype(F32), bias.reshape(-1, 1).astype(F32), tr)

    counts = cnt[:, 0].astype(jnp.int32)
    padded = (counts + te - 1) // te * te
    pad_end = jnp.cumsum(padded)
    pad_start = pad_end - padded
    dest = _slots(pad_start, idx, rank, min(t, 4096))
    used = pad_end[-1] // te
    blk = jnp.minimum(jnp.arange(n_blocks, dtype=jnp.int32), used - 1)
    exp = jnp.sum((pad_end[None, :] // te <= blk[:, None]).astype(jnp.int32), axis=1)
    exp = jnp.minimum(exp, N_EXPERTS - 1)

    xs = _dispatch(dest, x2, sorted_buf, td)
    ys = _experts(blk, exp.astype(jnp.int32), used.reshape(1), xs.reshape(n_rows * X_SLAB, LANES),
                  w_gu.astype(BF16), w_d.astype(BF16), te)
    out = _combine(dest, gate_tm, x2, ys, ws_gu.astype(BF16), ws_d.astype(BF16),
                   ln_g.reshape(1, -1), ln_b.reshape(1, -1), tc)
    return out, xs


def kernel(x, attn_w_in, attn_w_out, attn_lambda, attn_subln, hgrn_w_in, hgrn_w_out, hgrn_lower_bounds, hgrn_norm, moe_router, moe_router_bias, moe_w_gate_up, moe_w_down, shared_w_gate_up, shared_w_down, ln_gain, ln_bias):
    bsz, seq, _ = x.shape
    tables = _rotary_tables(seq)
    lb = jax.nn.softmax(hgrn_lower_bounds.astype(F32), axis=0)
    lb = jnp.cumsum(lb, axis=0) - lb[0]
    x2 = x.reshape(bsz * seq, D_MODEL)
    sorted_buf = None
    for layer in range(DEPTH):
        j = layer // 2
        if layer % 2 == 0:
            lambda_init = 0.8 - 0.6 * math.exp(-0.3 * layer)
            x2 = _attn_layer(x2, bsz, seq, attn_w_in[j], attn_w_out[j], attn_lambda[j], attn_subln[j],
                             lambda_init, ln_gain[layer, 0], ln_bias[layer, 0], tables)
        else:
            x2 = _hgrn_layer(x2, bsz, seq, hgrn_w_in[j], hgrn_w_out[j], lb[layer], hgrn_norm[j],
                             ln_gain[layer, 0], ln_bias[layer, 0])
        x2, sorted_buf = _moe_layer(x2, moe_router[layer], moe_router_bias[layer], moe_w_gate_up[layer],
                                    moe_w_down[layer], shared_w_gate_up[layer], shared_w_down[layer],
                                    ln_gain[layer, 1], ln_bias[layer, 1], sorted_buf)
    return x2.reshape(bsz, seq, D_MODEL)
```

```python
import functools
import math

import jax
import jax.numpy as jnp
from jax import lax
from jax.experimental import pallas as pl
from jax.experimental.pallas import tpu as pltpu

F32 = jnp.float32
BF16 = jnp.bfloat16

D_MODEL = 1024
DEPTH = 4
CHUNK = 64
A_HEADS = 8
A_HEAD_DIM = 64
ROT_DIM = 16
ROPE_THETA = 500000.0
H_HEADS = 8
H_EXPAND = 128
N_EXPERTS = 64
N_GROUPS = 8
TOPK_GROUP = 4
TOP_K = 8
EXPERT_FF = 256
ROUTED_SCALE = 2.5
ALPHA = (2 * DEPTH) ** 0.25
LN_EPS = 1e-5
LANES = 128
VMEM_LIMIT = 48 * 1024 * 1024


def _cparams(sem):
    return pltpu.CompilerParams(dimension_semantics=sem, vmem_limit_bytes=VMEM_LIMIT)


def _layer_norm(y, g, b):
    mu = jnp.mean(y, axis=-1, keepdims=True)
    yc = y - mu
    var = jnp.mean(yc * yc, axis=-1, keepdims=True)
    return yc * lax.rsqrt(var + LN_EPS) * g + b


def _dot_nt(a, b):
    return lax.dot_general(a, b, (((1,), (1,)), ((), ())), preferred_element_type=F32)


def _dot_tn(a, b):
    return lax.dot_general(a, b, (((0,), (0,)), ((), ())), preferred_element_type=F32)


def _attn_proj_kernel(x_ref, w_ref, cf_ref, s1_ref, s2_ref, q_ref, k_ref, v_ref):
    xb = x_ref[...].astype(BF16)
    cf, s1, s2 = cf_ref[...], s1_ref[...], s2_ref[...]

    def rotary(y):
        outs = []
        for c in range(D_MODEL // LANES):
            yc = y[:, c * LANES:(c + 1) * LANES]
            outs.append(yc * cf + pltpu.roll(yc, LANES - ROT_DIM // 2, 1) * s1
                        + pltpu.roll(yc, ROT_DIM // 2, 1) * s2)
        return jnp.concatenate(outs, axis=1)

    q = jnp.dot(xb, w_ref[:, 0:D_MODEL], preferred_element_type=F32)
    q_ref[...] = (rotary(q) * (A_HEAD_DIM ** -0.5 * math.log2(math.e))).astype(BF16)
    k = jnp.dot(xb, w_ref[:, D_MODEL:2 * D_MODEL], preferred_element_type=F32)
    k_ref[...] = rotary(k).astype(BF16)
    v = jnp.dot(xb, w_ref[:, 2 * D_MODEL:3 * D_MODEL], preferred_element_type=F32)
    v_ref[...] = v.astype(BF16)


def _attn_proj(x2, w, cf, s1, s2, seq, tm):
    t = x2.shape[0]
    nseq = seq // tm
    row = lambda i: (i, 0)
    tab = lambda i: (i % nseq, 0)
    out = jax.ShapeDtypeStruct((t, D_MODEL), BF16)
    return pl.pallas_call(
        _attn_proj_kernel,
        grid=(t // tm,),
        in_specs=[pl.BlockSpec((tm, D_MODEL), row),
                  pl.BlockSpec((D_MODEL, 3 * D_MODEL), lambda i: (0, 0)),
                  pl.BlockSpec((tm, LANES), tab), pl.BlockSpec((tm, LANES), tab),
                  pl.BlockSpec((tm, LANES), tab)],
        out_specs=[pl.BlockSpec((tm, D_MODEL), row)] * 3,
        out_shape=[out, out, out],
        compiler_params=_cparams(("arbitrary",)),
        name="attn_proj",
    )(x2, w, cf, s1, s2)


ATTN_WIDE = 4


def _attn_kernel(lam_ref, q_ref, k_ref, v_ref, g_ref, o_ref, m1_ref, a1_ref, m2_ref, a2_ref,
                 *, tq, out_scale):
    i = pl.program_id(2)
    q = q_ref[0]
    lane = lax.broadcasted_iota(jnp.int32, q.shape, 1)
    zero = jnp.zeros_like(q)
    q1 = jnp.where(lane < A_HEAD_DIM, q, zero)
    q2 = jnp.where(lane >= A_HEAD_DIM, q, zero)
    q12 = jnp.concatenate([q1, q2], axis=0)

    for m_ref, a_ref in ((m1_ref, a1_ref), (m2_ref, a2_ref)):
        m_ref[...] = jnp.full(m_ref.shape, -jnp.inf, F32)
        a_ref[...] = jnp.zeros(a_ref.shape, F32)

    def update(s, vb1, m_ref, a_ref):
        tiles = [s[:, c * LANES:(c + 1) * LANES] for c in range(s.shape[1] // LANES)]
        mc = functools.reduce(jnp.maximum, tiles)
        m_old = m_ref[...]
        m_new = jnp.maximum(m_old, jnp.max(mc, axis=-1, keepdims=True))
        alpha = jnp.exp2(m_old - m_new)
        pb = jnp.concatenate([jnp.exp2((t - m_new).astype(BF16)) for t in tiles], axis=1)
        pv = jnp.dot(pb, vb1, preferred_element_type=F32)
        a_ref[:, :LANES] = alpha * a_ref[:, :LANES] + pv[:, :LANES]
        a_ref[:, LANES:] = alpha * a_ref[:, LANES:] + pv[:, LANES:]
        m_ref[...] = m_new

    def block(j, width, diagonal):
        start = pl.multiple_of(j * tq, tq)
        kb = k_ref[0, pl.ds(start, width * tq), :]
        vb1 = jnp.concatenate([v_ref[0, pl.ds(start, width * tq), :],
                               jnp.ones((width * tq, LANES), BF16)], axis=1)
        s = _dot_nt(q12, kb)
        s1, s2 = s[:tq], s[tq:]
        if diagonal:
            rq = lax.broadcasted_iota(jnp.int32, (tq, width * tq), 0) // CHUNK
            ck = lax.broadcasted_iota(jnp.int32, (tq, width * tq), 1) // CHUNK
            mask = ck <= rq + (width - 1) * (tq // CHUNK)
            s1 = jnp.where(mask, s1, -jnp.inf)
            s2 = jnp.where(mask, s2, -jnp.inf)
        update(s1, vb1, m1_ref, a1_ref)
        update(s2, vb1, m2_ref, a2_ref)

    def wide_block(p, carry):
        block(ATTN_WIDE * p, ATTN_WIDE, False)
        return carry

    lax.fori_loop(0, i // ATTN_WIDE, wide_block, 0)
    for rem in range(ATTN_WIDE):
        @pl.when(i % ATTN_WIDE == rem)
        def _(rem=rem):
            block(i - rem, rem + 1, True)


    lam = lam_ref[0]
    o = a1_ref[:, :LANES] / a1_ref[:, LANES:] - lam * (a2_ref[:, :LANES] / a2_ref[:, LANES:])
    o = o * lax.rsqrt(jnp.mean(o * o, axis=-1, keepdims=True) + LN_EPS) * g_ref[...]
    o_ref[0] = (o * out_scale).astype(o_ref.dtype)


def _attention(lam, q, k, v, g, lambda_init, tq):
    b, s, _ = q.shape
    kern = functools.partial(_attn_kernel, tq=tq, out_scale=1.0 - lambda_init)
    kv_spec = pl.BlockSpec((1, s, 2 * A_HEAD_DIM), lambda bi, h, i: (bi, 0, h))
    qo_spec = pl.BlockSpec((1, tq, 2 * A_HEAD_DIM), lambda bi, h, i: (bi, i, h))
    vec = lambda n: pltpu.VMEM((tq, n), F32)
    return pl.pallas_call(
        kern,
        grid=(b, A_HEADS, s // tq),
        in_specs=[pl.BlockSpec(memory_space=pltpu.SMEM), qo_spec, kv_spec, kv_spec,
                  pl.BlockSpec((1, 2 * A_HEAD_DIM), lambda bi, h, i: (0, 0))],
        out_specs=qo_spec,
        out_shape=jax.ShapeDtypeStruct((b, s, D_MODEL), BF16),
        scratch_shapes=[vec(LANES), vec(2 * LANES)] * 2,
        compiler_params=_cparams(("arbitrary", "arbitrary", "arbitrary")),
        name="diff_attention",
    )(lam, q, k, v, g)


def _out_proj_kernel(o_ref, x_ref, w_ref, g_ref, b_ref, y_ref):
    h = jnp.dot(o_ref[...], w_ref[...], preferred_element_type=F32)
    y_ref[...] = _layer_norm(ALPHA * x_ref[...] + h, g_ref[...], b_ref[...])


def _out_proj(o2, x2, w, g, b, tm):
    t = x2.shape[0]
    row = lambda i: (i, 0)
    fixed = lambda i: (0, 0)
    return pl.pallas_call(
        _out_proj_kernel,
        grid=(t // tm,),
        in_specs=[pl.BlockSpec((tm, D_MODEL), row), pl.BlockSpec((tm, D_MODEL), row),
                  pl.BlockSpec((D_MODEL, D_MODEL), fixed),
                  pl.BlockSpec((1, D_MODEL), fixed), pl.BlockSpec((1, D_MODEL), fixed)],
        out_specs=pl.BlockSpec((tm, D_MODEL), row),
        out_shape=jax.ShapeDtypeStruct((t, D_MODEL), F32),
        compiler_params=_cparams(("arbitrary",)),
        name="out_proj_ln",
    )(o2, x2, w, g, b)


def _rotary_tables(seq):
    pos = jnp.arange(seq, dtype=F32)
    inv_freq = ROPE_THETA ** (-jnp.arange(0, ROT_DIM, 2, dtype=F32) / ROT_DIM)
    ang = pos[:, None] * inv_freq[None, :]
    cos, sin = jnp.cos(ang), jnp.sin(ang)
    half = ROT_DIM // 2
    pad = A_HEAD_DIM - ROT_DIM
    one = jnp.ones((seq, pad), F32)
    zero = jnp.zeros((seq, pad), F32)
    zh = jnp.zeros((seq, half), F32)
    cf = jnp.concatenate([cos, cos, one], axis=1)
    s1 = jnp.concatenate([-sin, zh, zero], axis=1)
    s2 = jnp.concatenate([zh, sin, zero], axis=1)
    rep = LANES // A_HEAD_DIM
    return tuple(jnp.tile(t, (1, rep)) for t in (cf, s1, s2))


def _attn_layer(x2, bsz, seq, w_in, w_out, lam_params, subln, lambda_init, ln_g, ln_b, tables,
                tm=512, tq=512):
    q, k, v = _attn_proj(x2, w_in.astype(BF16), *tables, seq, tm)
    lp = lam_params.astype(F32)
    lam = jnp.exp(jnp.sum(lp[0] * lp[1])) - jnp.exp(jnp.sum(lp[2] * lp[3])) + lambda_init
    shp = (bsz, seq, D_MODEL)
    o = _attention(lam.reshape(1), q.reshape(shp), k.reshape(shp), v.reshape(shp),
                   subln.reshape(1, -1).astype(F32), lambda_init, tq)
    return _out_proj(o.reshape(bsz * seq, D_MODEL), x2, w_out.astype(BF16),
                     ln_g.reshape(1, -1), ln_b.reshape(1, -1), tm)


H_CHUNK = 128
H_SUB = 16


def _hgrn_proj_kernel(x_ref, w_ref, lb_ref, q_ref, k_ref, v_ref, lf_ref, g_ref):
    xb = x_ref[...].astype(BF16)
    w = H_HEADS * H_EXPAND
    q = jnp.dot(xb, w_ref[:, 0:w], preferred_element_type=F32)
    q_ref[...] = q * jax.nn.sigmoid(q)
    f = jnp.dot(xb, w_ref[:, w:2 * w], preferred_element_type=F32)
    lb = lb_ref[...]
    forget = lb + (1.0 - lb) * jax.nn.sigmoid(f)
    lf_ref[...] = jnp.log(forget)
    k_ref[...] = 1.0 - forget
    v_ref[...] = jnp.dot(xb, w_ref[:, 2 * w:3 * w], preferred_element_type=F32)
    g = jnp.dot(xb, w_ref[:, 3 * w:4 * w], preferred_element_type=F32)
    g_ref[...] = g * jax.nn.sigmoid(g)


def _hgrn_proj(x2, w, lb, tm):
    t = x2.shape[0]
    row = lambda i: (i, 0)
    fixed = lambda i: (0, 0)
    out = jax.ShapeDtypeStruct((t, D_MODEL), F32)
    return pl.pallas_call(
        _hgrn_proj_kernel,
        grid=(t // tm,),
        in_specs=[pl.BlockSpec((tm, D_MODEL), row), pl.BlockSpec((D_MODEL, 4 * D_MODEL), fixed),
                  pl.BlockSpec((1, D_MODEL), fixed)],
        out_specs=[pl.BlockSpec((tm, D_MODEL), row)] * 5,
        out_shape=[out] * 5,
        compiler_params=_cparams(("arbitrary",)),
        name="hgrn_proj",
    )(x2, w, lb)


def _hgrn_chunk(q, k, v, lf, st):
    c, nb = H_CHUNK, H_CHUNK // H_SUB
    r_i = lax.broadcasted_iota(jnp.int32, (c, c), 0)
    c_i = lax.broadcasted_iota(jnp.int32, (c, c), 1)
    tri = (r_i >= c_i).astype(BF16)
    hi = lf.astype(BF16)
    lo = (lf - hi.astype(F32)).astype(BF16)
    b2 = jnp.dot(tri, jnp.concatenate([hi, lo], axis=1), preferred_element_type=F32)
    b = (b2[:, :H_EXPAND] + b2[:, H_EXPAND:]) * math.log2(math.e)
    b_last = b[c - 1:c, :]

    o = _dot_nt((q * jnp.exp2(b)).astype(BF16), st.astype(BF16))
    ke = k * jnp.exp2(b_last - b)
    st_new = st * jnp.exp2(b_last) + _dot_tn(v.astype(BF16), ke.astype(BF16))

    ends = [b[(j + 1) * H_SUB - 1:(j + 1) * H_SUB, :] for j in range(nb)]
    e_blk = jnp.concatenate([jnp.broadcast_to(e, (H_SUB, H_EXPAND)) for e in ends], axis=0)
    kt = k * jnp.exp2(e_blk - b)
    row_blk = lax.broadcasted_iota(jnp.int32, (c, H_EXPAND), 0) // H_SUB
    q_big = jnp.concatenate(
        [(q * jnp.exp2(jnp.minimum(b - ends[j], 0.0))).astype(BF16) for j in range(nb)], axis=1)
    k_big = jnp.concatenate(
        [jnp.where(row_blk == j, kt, 0.0).astype(BF16) for j in range(nb)], axis=1)
    a = _dot_nt(q_big, k_big)
    a = jnp.where(r_i // H_SUB > c_i // H_SUB, a, 0.0)
    o = o + jnp.dot(a.astype(BF16), v.astype(BF16), preferred_element_type=F32)

    t_idx = lax.broadcasted_iota(jnp.int32, (H_SUB, 1), 0)
    diag = []
    for j in range(nb):
        sl = slice(j * H_SUB, (j + 1) * H_SUB)
        qb, kb, vb, bb = q[sl], k[sl], v[sl], b[sl]
        od = jnp.zeros((H_SUB, H_EXPAND), F32)
        for s in range(H_SUB):
            e = jnp.exp2(bb - bb[s:s + 1, :])
            col = jnp.sum(qb * (kb[s:s + 1, :] * e), axis=-1, keepdims=True)
            od = od + jnp.where(t_idx >= s, col, 0.0) * vb[s:s + 1, :]
        diag.append(od)
    return o + jnp.concatenate(diag, axis=0), st_new


def _hgrn_kernel(q_ref, k_ref, v_ref, lf_ref, g_ref, ng_ref, o_ref, st_ref, *, n_chunks):
    @pl.when(pl.program_id(2) == 0)
    def _():
        st_ref[...] = jnp.zeros(st_ref.shape, F32)

    st = st_ref[...]
    for ci in range(n_chunks):
        sl = slice(ci * H_CHUNK, (ci + 1) * H_CHUNK)
        o, st = _hgrn_chunk(q_ref[sl, :], k_ref[sl, :], v_ref[sl, :], lf_ref[sl, :], st)
        o = o * lax.rsqrt(jnp.mean(o * o, axis=-1, keepdims=True) + LN_EPS) * ng_ref[...]
        o_ref[sl, :] = (o * g_ref[sl, :]).astype(o_ref.dtype)
    st_ref[...] = st


def _hgrn_recurrence(q, k, v, lf, g, ng, bsz, seq, tc):
    t = q.shape[0]
    nt = seq // tc
    blk = pl.BlockSpec((tc, H_EXPAND), lambda b, h, c: (b * nt + c, h))
    return pl.pallas_call(
        functools.partial(_hgrn_kernel, n_chunks=tc // H_CHUNK),
        grid=(bsz, H_HEADS, nt),
        in_specs=[blk] * 5 + [pl.BlockSpec((1, H_EXPAND), lambda b, h, c: (0, 0))],
        out_specs=blk,
        out_shape=jax.ShapeDtypeStruct((t, D_MODEL), BF16),
        scratch_shapes=[pltpu.VMEM((H_EXPAND, H_EXPAND), F32)],
        compiler_params=_cparams(("arbitrary", "arbitrary", "arbitrary")),
        name="hgrn_recurrence",
    )(q, k, v, lf, g, ng)


def _hgrn_layer(x2, bsz, seq, w_in, w_out, lb, norm_g, ln_g, ln_b, tm=256, tc=512):
    q, k, v, lf, g = _hgrn_proj(x2, w_in.astype(BF16), lb.reshape(1, -1).astype(F32), tm)
    o = _hgrn_recurrence(q, k, v, lf, g, norm_g.reshape(1, -1).astype(F32), bsz, seq, tc)
    return _out_proj(o, x2, w_out.astype(BF16), ln_g.reshape(1, -1), ln_b.reshape(1, -1), 2 * tm)


GROUP_SIZE = N_EXPERTS // N_GROUPS
X_WORDS = D_MODEL // 2
X_SLAB = X_WORDS // LANES


def _router_kernel(x_ref, wt_ref, bias_ref, idx_ref, rank_ref, gate_tm_ref, cnt_ref,
                   carry_ref, *, tr):
    @pl.when(pl.program_id(0) == 0)
    def _():
        carry_ref[...] = jnp.zeros(carry_ref.shape, F32)

    e_n = N_EXPERTS
    logits = lax.dot_general(wt_ref[...], x_ref[...], (((1,), (1,)), ((), ())),
                             precision=lax.Precision.HIGHEST, preferred_element_type=F32)
    scores = jax.nn.sigmoid(logits)
    choice = scores + bias_ref[...]

    ch3 = choice.reshape(N_GROUPS, GROUP_SIZE, tr)
    sub = lax.broadcasted_iota(jnp.int32, ch3.shape, 1)
    m1 = jnp.max(ch3, axis=1, keepdims=True)
    first = jnp.min(jnp.where(ch3 == m1, sub, GROUP_SIZE), axis=1, keepdims=True)
    m2 = jnp.max(jnp.where(sub == first, -jnp.inf, ch3), axis=1, keepdims=True)
    gs = (m1 + m2).reshape(N_GROUPS, tr)

    g_i = lax.broadcasted_iota(jnp.int32, gs.shape, 0)
    g_rank = jnp.zeros(gs.shape, F32)
    for g in range(N_GROUPS):
        row = gs[g:g + 1, :]
        ahead = (row > gs) | ((row == gs) & (g_i > g))
        g_rank = g_rank + jnp.where(ahead, 1.0, 0.0)
    g_keep = (g_rank < TOPK_GROUP).astype(F32).reshape(N_GROUPS, 1, tr)
    keep = jnp.broadcast_to(g_keep, (N_GROUPS, GROUP_SIZE, tr)).reshape(e_n, tr) > 0.5
    cm = jnp.where(keep, choice, -jnp.inf)

    e_i = lax.broadcasted_iota(jnp.int32, cm.shape, 0)
    sub_i = lax.broadcasted_iota(jnp.int32, (GROUP_SIZE, tr), 0)
    cm_g = [cm[g * GROUP_SIZE:(g + 1) * GROUP_SIZE, :] for g in range(N_GROUPS)]
    ranks = [jnp.zeros((GROUP_SIZE, tr), F32) for _ in range(N_GROUPS)]
    for e in range(e_n):
        ge, se = divmod(e, GROUP_SIZE)
        row = jnp.broadcast_to(cm_g[ge][se:se + 1, :], (GROUP_SIZE, tr))
        for g in range(N_GROUPS):
            if g > ge:
                ahead = row >= cm_g[g]
            elif g < ge:
                ahead = row > cm_g[g]
            else:
                ahead = (row > cm_g[g]) | ((row == cm_g[g]) & (sub_i > se))
            ranks[g] = ranks[g] + jnp.where(ahead, 1.0, 0.0)
    sel = jnp.concatenate(ranks, axis=0) < TOP_K
    sel_f = sel.astype(F32)
    w = jnp.where(sel, scores, 0.0)
    gate = w / (jnp.sum(w, axis=0, keepdims=True) + 1e-20) * ROUTED_SCALE

    sel_b = sel_f.astype(BF16)
    t_r = lax.broadcasted_iota(jnp.int32, (tr, tr), 0)
    t_c = lax.broadcasted_iota(jnp.int32, (tr, tr), 1)
    before = jnp.dot(sel_b, (t_r < t_c).astype(BF16), preferred_element_type=F32)
    tok_rank = carry_ref[:, 0:1] + before
    carry_ref[...] = carry_ref[...] + jnp.sum(sel_f, axis=1, keepdims=True)
    cnt_ref[...] = carry_ref[...]

    x_r = lax.broadcasted_iota(jnp.int32, (e_n, e_n), 0)
    x_c = lax.broadcasted_iota(jnp.int32, (e_n, e_n), 1)
    slot = jnp.dot((x_c < x_r).astype(BF16), sel_b, preferred_element_type=F32)
    e_f = e_i.astype(F32)
    idx_rows, gate_rows, rank_rows = [], [], []
    for j in range(TOP_K):
        pick = jnp.where(sel & (slot == j), 1.0, 0.0)
        idx_rows.append(jnp.sum(pick * e_f, axis=0, keepdims=True))
        gate_rows.append(jnp.sum(pick * gate, axis=0, keepdims=True))
        rank_rows.append(jnp.sum(pick * tok_rank, axis=0, keepdims=True))
    idx_ref[...] = jnp.concatenate(idx_rows, axis=0).astype(jnp.int32)
    rank_ref[...] = jnp.concatenate(rank_rows, axis=0).astype(jnp.int32)
    gates = jnp.concatenate(gate_rows, axis=0)
    padded = jnp.concatenate([gates, jnp.zeros((LANES - TOP_K, tr), F32)], axis=0)
    gate_tm_ref[...] = padded.T


def _router(x2, wt, bias, tr):
    t = x2.shape[0]
    col = lambda i: (0, i)
    return pl.pallas_call(
        functools.partial(_router_kernel, tr=tr),
        grid=(t // tr,),
        in_specs=[pl.BlockSpec((tr, D_MODEL), lambda i: (i, 0)),
                  pl.BlockSpec((N_EXPERTS, D_MODEL), lambda i: (0, 0)),
                  pl.BlockSpec((N_EXPERTS, 1), lambda i: (0, 0))],
        out_specs=[pl.BlockSpec((TOP_K, tr), col),
                   pl.BlockSpec((TOP_K, tr), col), pl.BlockSpec((tr, LANES), lambda i: (i, 0)),
                   pl.BlockSpec((N_EXPERTS, LANES), lambda i: (0, 0))],
        out_shape=[jax.ShapeDtypeStruct((TOP_K, t), jnp.int32),
                   jax.ShapeDtypeStruct((TOP_K, t), jnp.int32),
                   jax.ShapeDtypeStruct((t, LANES), F32),
                   jax.ShapeDtypeStruct((N_EXPERTS, LANES), F32)],
        scratch_shapes=[pltpu.VMEM((N_EXPERTS, LANES), F32)],
        compiler_params=_cparams(("arbitrary",)),
        name="moe_router",
    )(x2, wt, bias)


ISSUE_UNROLL = 8


def _pack_rows(x):
    lo = pltpu.bitcast(x[:, :X_WORDS].astype(BF16).astype(F32), jnp.uint32)
    hi = pltpu.bitcast(x[:, X_WORDS:].astype(BF16).astype(F32), jnp.uint32)
    return (lo >> 16) | (hi & jnp.uint32(0xFFFF0000))


def _unpack_words(words):
    return (pltpu.bitcast(words << 16, F32), pltpu.bitcast(words & jnp.uint32(0xFFFF0000), F32))


def _slots_kernel(start_ref, idx_ref, rank_ref, dest_ref):
    idx = idx_ref[...]
    dest = rank_ref[...]
    for e in range(N_EXPERTS):
        dest = dest + jnp.where(idx == e, start_ref[e], 0)
    dest_ref[...] = dest


def _slots(pad_start, idx, rank, tl):
    t = idx.shape[1]
    col = pl.BlockSpec((TOP_K, tl), lambda i: (0, i))
    return pl.pallas_call(
        _slots_kernel,
        grid=(t // tl,),
        in_specs=[pl.BlockSpec(memory_space=pltpu.SMEM), col, col],
        out_specs=col,
        out_shape=jax.ShapeDtypeStruct(idx.shape, jnp.int32),
        compiler_params=_cparams(("arbitrary",)),
        name="moe_slots",
    )(pad_start, idx, rank)


def _dispatch_kernel(dest_ref, x_ref, zeros_ref, xs_ref, xp_ref, sem, *, td):
    del zeros_ref
    words = _pack_rows(x_ref[...])
    for j in range(X_SLAB):
        xp_ref[:, j, :] = words[:, j * LANES:(j + 1) * LANES]

    def issue(g, carry):
        for u in range(ISSUE_UNROLL):
            t = g * ISSUE_UNROLL + u
            for j in range(TOP_K):
                pltpu.make_async_copy(xp_ref.at[t], xs_ref.at[dest_ref[j, t]], sem).start(priority=j % 2)
        return carry

    lax.fori_loop(0, td // ISSUE_UNROLL, issue, 0)
    for j in range(TOP_K):
        pltpu.make_async_copy(xp_ref, xs_ref.at[pl.ds(0, td)], sem).wait()


def _dispatch(dest, x2, zeros, td):
    t = x2.shape[0]
    return pl.pallas_call(
        functools.partial(_dispatch_kernel, td=td),
        grid=(t // td,),
        in_specs=[pl.BlockSpec((TOP_K, td), lambda i: (0, i), memory_space=pltpu.SMEM),
                  pl.BlockSpec((td, D_MODEL), lambda i: (i, 0)),
                  pl.BlockSpec(memory_space=pl.ANY)],
        out_specs=pl.BlockSpec(memory_space=pl.ANY),
        out_shape=jax.ShapeDtypeStruct(zeros.shape, zeros.dtype),
        input_output_aliases={2: 0},
        scratch_shapes=[pltpu.VMEM((td, X_SLAB, LANES), jnp.uint32), pltpu.SemaphoreType.DMA],
        compiler_params=_cparams(("arbitrary",)),
        name="moe_dispatch",
    )(dest, x2, zeros)


def _unpack_rows(words):
    return jnp.concatenate(_unpack_words(words), axis=1).astype(BF16)


def _swiglu(xb, w_gu, w_d):
    h = jnp.dot(xb, w_gu, preferred_element_type=F32)
    act = h[:, :EXPERT_FF] * jax.nn.sigmoid(h[:, :EXPERT_FF]) * h[:, EXPERT_FF:]
    return jnp.dot(act.astype(BF16), w_d, preferred_element_type=F32)


def _expert_kernel(blk_ref, exp_ref, used_ref, xs_ref, wgu_ref, wd_ref, ys_ref, wgu_b, wd_b, *, te):
    del blk_ref
    b = pl.program_id(0)
    live = b < used_ref[0]

    @pl.when(jnp.logical_or(b == 0, exp_ref[b] != exp_ref[jnp.maximum(b - 1, 0)]))
    def _():
        wgu_b[...] = wgu_ref[0].astype(BF16)
        wd_b[...] = wd_ref[0].astype(BF16)

    @pl.when(live)
    def _():
        words = jnp.concatenate(
            [xs_ref[pl.ds(j, te, stride=X_SLAB), :] for j in range(X_SLAB)], axis=1)
        y = _pack_rows(_swiglu(_unpack_rows(words), wgu_b[...], wd_b[...]))
        for c in range(X_SLAB):
            ys_ref[pl.ds(c, te, stride=X_SLAB), :] = y[:, c * LANES:(c + 1) * LANES]

    @pl.when(jnp.logical_not(live))
    def _():
        ys_ref[...] = jnp.zeros(ys_ref.shape, ys_ref.dtype)


def _experts(blk, exp, used, xs2, w_gu, w_d, te):
    n_blocks = blk.shape[0]
    n_rows = xs2.shape[0] // X_SLAB
    return pl.pallas_call(
        functools.partial(_expert_kernel, te=te),
        grid_spec=pltpu.PrefetchScalarGridSpec(
            num_scalar_prefetch=3,
            grid=(n_blocks,),
            in_specs=[pl.BlockSpec((te * X_SLAB, LANES), lambda b, blk, exp, used: (blk[b], 0)),
                      pl.BlockSpec((1, D_MODEL, 2 * EXPERT_FF), lambda b, blk, exp, used: (exp[b], 0, 0)),
                      pl.BlockSpec((1, EXPERT_FF, D_MODEL), lambda b, blk, exp, used: (exp[b], 0, 0))],
            out_specs=pl.BlockSpec((te * X_SLAB, LANES), lambda b, blk, exp, used: (b, 0)),
            scratch_shapes=[pltpu.VMEM((D_MODEL, 2 * EXPERT_FF), BF16),
                            pltpu.VMEM((EXPERT_FF, D_MODEL), BF16)],
        ),
        out_shape=jax.ShapeDtypeStruct((n_rows * X_SLAB, LANES), jnp.uint32),
        compiler_params=_cparams(("arbitrary",)),
        name="moe_experts",
    )(blk, exp, used, xs2, w_gu, w_d)


def _combine_kernel(dest_ref, dest_next_ref, gate_ref, x_ref, wgu_ref, wd_ref, g_ref, b_ref, ys_ref, ys2_ref,
                    o_ref, ybuf_ref, sem, *, tc):
    i = pl.program_id(0)
    last = pl.num_programs(0) - 1
    slot = i % 2
    rows = TOP_K * tc

    def row_copy(d_ref, j, t, into):
        dst = ybuf_ref.at[pl.ds((into * rows + j * tc + t) * X_SLAB, X_SLAB)]
        return pltpu.make_async_copy(ys_ref.at[d_ref[j, t]], dst, sem.at[into])

    def wait_tile(into):
        pltpu.make_async_copy(ys2_ref.at[pl.ds(0, rows * X_SLAB)],
                              ybuf_ref.at[pl.ds(into * rows * X_SLAB, rows * X_SLAB)], sem.at[into]).wait()

    @pl.when(i == 0)
    def _():
        def issue(t, carry):
            for j in range(TOP_K):
                row_copy(dest_ref, j, t, 0).start(priority=j % 2)
            return carry
        lax.fori_loop(0, tc, issue, 0)

    wait_tile(slot)
    x = x_ref[...]
    gate = gate_ref[...]
    acc_lo = [None] * X_SLAB
    acc_hi = [None] * X_SLAB
    per_phase = tc // TOP_K
    for j in range(TOP_K):
        for t in range(j * per_phase, (j + 1) * per_phase):
            for jj in range(TOP_K):
                row_copy(dest_next_ref, jj, t, 1 - slot).start(priority=jj % 2)
        g_j = gate[:, j:j + 1]
        for c in range(X_SLAB):
            start = (slot * rows + j * tc) * X_SLAB + c
            lo, hi = _unpack_words(ybuf_ref[pl.ds(start, tc, stride=X_SLAB), :])
            acc_lo[c] = g_j * lo if j == 0 else acc_lo[c] + g_j * lo
            acc_hi[c] = g_j * hi if j == 0 else acc_hi[c] + g_j * hi
    routed = jnp.concatenate(acc_lo + acc_hi, axis=1)
    acc = ALPHA * x + routed + _swiglu(x.astype(BF16), wgu_ref[...], wd_ref[...])
    o_ref[...] = _layer_norm(acc, g_ref[...], b_ref[...])

    @pl.when(i == last)
    def _():
        wait_tile(1 - slot)


def _combine(dest, gate_tm, x2, ys, ws_gu, ws_d, g, b, tc):
    t = x2.shape[0]
    n = t // tc
    fixed = lambda i: (0, 0)
    row = lambda i: (i, 0)
    smem = lambda f: pl.BlockSpec((TOP_K, tc), f, memory_space=pltpu.SMEM)
    return pl.pallas_call(
        functools.partial(_combine_kernel, tc=tc),
        grid=(n,),
        in_specs=[smem(lambda i: (0, i)), smem(lambda i: (0, jnp.minimum(i + 1, n - 1))),
                  pl.BlockSpec((tc, LANES), row), pl.BlockSpec((tc, D_MODEL), row),
                  pl.BlockSpec((D_MODEL, 2 * EXPERT_FF), fixed), pl.BlockSpec((EXPERT_FF, D_MODEL), fixed),
                  pl.BlockSpec((1, D_MODEL), fixed), pl.BlockSpec((1, D_MODEL), fixed),
                  pl.BlockSpec(memory_space=pl.ANY), pl.BlockSpec(memory_space=pl.ANY)],
        out_specs=pl.BlockSpec((tc, D_MODEL), row),
        out_shape=jax.ShapeDtypeStruct((t, D_MODEL), F32),
        scratch_shapes=[pltpu.VMEM((2 * TOP_K * tc * X_SLAB, LANES), jnp.uint32),
                        pltpu.SemaphoreType.DMA((2,))],
        compiler_params=_cparams(("arbitrary",)),
        name="moe_combine",
    )(dest, dest, gate_tm, x2, ws_gu, ws_d, g, b, ys.reshape(-1, X_SLAB, LANES), ys)


def _moe_layer(x2, w_router, bias, w_gu, w_d, ws_gu, ws_d, ln_g, ln_b, sorted_buf=None,
               te=512, tr=512, td=512, tc=128):
    t = x2.shape[0]
    n_blocks = t * TOP_K // te + N_EXPERTS
    n_rows = n_blocks * te
    if sorted_buf is None:
        sorted_buf = jnp.zeros((n_rows, X_SLAB, LANES), jnp.uint32)
    idx, rank, gate_tm, cnt = _router(x2, w_router.T.astype(F32), bias.reshape(-1, 1).astype(F32), tr)

    counts = cnt[:, 0].astype(jnp.int32)
    padded = (counts + te - 1) // te * te
    pad_end = jnp.cumsum(padded)
    pad_start = pad_end - padded
    dest = _slots(pad_start, idx, rank, min(t, 4096))
    used = pad_end[-1] // te
    blk = jnp.minimum(jnp.arange(n_blocks, dtype=jnp.int32), used - 1)
    exp = jnp.sum((pad_end[None, :] // te <= blk[:, None]).astype(jnp.int32), axis=1)
    exp = jnp.minimum(exp, N_EXPERTS - 1)

    xs = _dispatch(dest, x2, sorted_buf, td)
    ys = _experts(blk, exp.astype(jnp.int32), used.reshape(1), xs.reshape(n_rows * X_SLAB, LANES),
                  w_gu, w_d, te)
    out = _combine(dest, gate_tm, x2, ys, ws_gu.astype(BF16), ws_d.astype(BF16),
                   ln_g.reshape(1, -1), ln_b.reshape(1, -1), tc)
    return out, xs


def kernel(x, attn_w_in, attn_w_out, attn_lambda, attn_subln, hgrn_w_in, hgrn_w_out, hgrn_lower_bounds, hgrn_norm, moe_router, moe_router_bias, moe_w_gate_up, moe_w_down, shared_w_gate_up, shared_w_down, ln_gain, ln_bias):
    bsz, seq, _ = x.shape
    tables = _rotary_tables(seq)
    lb = jax.nn.softmax(hgrn_lower_bounds.astype(F32), axis=0)
    lb = jnp.cumsum(lb, axis=0) - lb[0]
    x2 = x.reshape(bsz * seq, D_MODEL)
    sorted_buf = None
    for layer in range(DEPTH):
        j = layer // 2
        if layer % 2 == 0:
            lambda_init = 0.8 - 0.6 * math.exp(-0.3 * layer)
            x2 = _attn_layer(x2, bsz, seq, attn_w_in[j], attn_w_out[j], attn_lambda[j], attn_subln[j],
                             lambda_init, ln_gain[layer, 0], ln_bias[layer, 0], tables)
        else:
            x2 = _hgrn_layer(x2, bsz, seq, hgrn_w_in[j], hgrn_w_out[j], lb[layer], hgrn_norm[j],
                             ln_gain[layer, 0], ln_bias[layer, 0])
        x2, sorted_buf = _moe_layer(x2, moe_router[layer], moe_router_bias[layer], moe_w_gate_up[layer],
                                    moe_w_down[layer], shared_w_gate_up[layer], shared_w_down[layer],
                                    ln_gain[layer, 1], ln_bias[layer, 1], sorted_buf)
    return x2.reshape(bsz, seq, D_MODEL)
```

```python
import functools
import math

import jax
import jax.numpy as jnp
from jax import lax
from jax.experimental import pallas as pl
from jax.experimental.pallas import tpu as pltpu

F32 = jnp.float32
BF16 = jnp.bfloat16

D_MODEL = 1024
DEPTH = 4
CHUNK = 64
A_HEADS = 8
A_HEAD_DIM = 64
ROT_DIM = 16
ROPE_THETA = 500000.0
H_HEADS = 8
H_EXPAND = 128
N_EXPERTS = 64
N_GROUPS = 8
TOPK_GROUP = 4
TOP_K = 8
EXPERT_FF = 256
ROUTED_SCALE = 2.5
ALPHA = (2 * DEPTH) ** 0.25
LN_EPS = 1e-5
LANES = 128
VMEM_LIMIT = 48 * 1024 * 1024


def _cparams(sem):
    return pltpu.CompilerParams(dimension_semantics=sem, vmem_limit_bytes=VMEM_LIMIT)


def _layer_norm(y, g, b):
    mu = jnp.mean(y, axis=-1, keepdims=True)
    yc = y - mu
    var = jnp.mean(yc * yc, axis=-1, keepdims=True)
    return yc * lax.rsqrt(var + LN_EPS) * g + b


def _dot_nt(a, b):
    return lax.dot_general(a, b, (((1,), (1,)), ((), ())), preferred_element_type=F32)


def _dot_tn(a, b):
    return lax.dot_general(a, b, (((0,), (0,)), ((), ())), preferred_element_type=F32)


def _attn_proj_kernel(x_ref, w_ref, cf_ref, s1_ref, s2_ref, q_ref, k_ref, v_ref):
    xb = x_ref[...].astype(BF16)
    cf, s1, s2 = cf_ref[...], s1_ref[...], s2_ref[...]

    def rotary(y):
        outs = []
        for c in range(D_MODEL // LANES):
            yc = y[:, c * LANES:(c + 1) * LANES]
            outs.append(yc * cf + pltpu.roll(yc, LANES - ROT_DIM // 2, 1) * s1
                        + pltpu.roll(yc, ROT_DIM // 2, 1) * s2)
        return jnp.concatenate(outs, axis=1)

    q = jnp.dot(xb, w_ref[:, 0:D_MODEL], preferred_element_type=F32)
    q_ref[...] = (rotary(q) * (A_HEAD_DIM ** -0.5 * math.log2(math.e))).astype(BF16)
    k = jnp.dot(xb, w_ref[:, D_MODEL:2 * D_MODEL], preferred_element_type=F32)
    k_ref[...] = rotary(k).astype(BF16)
    v = jnp.dot(xb, w_ref[:, 2 * D_MODEL:3 * D_MODEL], preferred_element_type=F32)
    v_ref[...] = v.astype(BF16)


def _attn_proj(x2, w, cf, s1, s2, seq, tm):
    t = x2.shape[0]
    nseq = seq // tm
    row = lambda i: (i, 0)
    tab = lambda i: (i % nseq, 0)
    out = jax.ShapeDtypeStruct((t, D_MODEL), BF16)
    return pl.pallas_call(
        _attn_proj_kernel,
        grid=(t // tm,),
        in_specs=[pl.BlockSpec((tm, D_MODEL), row),
                  pl.BlockSpec((D_MODEL, 3 * D_MODEL), lambda i: (0, 0)),
                  pl.BlockSpec((tm, LANES), tab), pl.BlockSpec((tm, LANES), tab),
                  pl.BlockSpec((tm, LANES), tab)],
        out_specs=[pl.BlockSpec((tm, D_MODEL), row)] * 3,
        out_shape=[out, out, out],
        compiler_params=_cparams(("arbitrary",)),
        name="attn_proj",
    )(x2, w, cf, s1, s2)


ATTN_WIDE = 4


def _attn_kernel(lam_ref, q_ref, k_ref, v_ref, g_ref, o_ref, m1_ref, a1_ref, m2_ref, a2_ref,
                 *, tq, out_scale):
    i = pl.program_id(2)
    q = q_ref[0]
    lane = lax.broadcasted_iota(jnp.int32, q.shape, 1)
    zero = jnp.zeros_like(q)
    q1 = jnp.where(lane < A_HEAD_DIM, q, zero)
    q2 = jnp.where(lane >= A_HEAD_DIM, q, zero)
    q12 = jnp.concatenate([q1, q2], axis=0)

    for m_ref, a_ref in ((m1_ref, a1_ref), (m2_ref, a2_ref)):
        m_ref[...] = jnp.full(m_ref.shape, -jnp.inf, F32)
        a_ref[...] = jnp.zeros(a_ref.shape, F32)

    def update(s, vb1, m_ref, a_ref):
        tiles = [s[:, c * LANES:(c + 1) * LANES] for c in range(s.shape[1] // LANES)]
        mc = functools.reduce(jnp.maximum, tiles)
        m_old = m_ref[...]
        m_new = jnp.maximum(m_old, jnp.max(mc, axis=-1, keepdims=True))
        alpha = jnp.exp2(m_old - m_new)
        pb = jnp.concatenate([jnp.exp2((t - m_new).astype(BF16)) for t in tiles], axis=1)
        pv = jnp.dot(pb, vb1, preferred_element_type=F32)
        a_ref[:, :LANES] = alpha * a_ref[:, :LANES] + pv[:, :LANES]
        a_ref[:, LANES:] = alpha * a_ref[:, LANES:] + pv[:, LANES:]
        m_ref[...] = m_new

    def block(j, width, diagonal):
        start = pl.multiple_of(j * tq, tq)
        kb = k_ref[0, pl.ds(start, width * tq), :]
        vb1 = jnp.concatenate([v_ref[0, pl.ds(start, width * tq), :],
                               jnp.ones((width * tq, LANES), BF16)], axis=1)
        s = _dot_nt(q12, kb)
        s1, s2 = s[:tq], s[tq:]
        if diagonal:
            rq = lax.broadcasted_iota(jnp.int32, (tq, width * tq), 0) // CHUNK
            ck = lax.broadcasted_iota(jnp.int32, (tq, width * tq), 1) // CHUNK
            mask = ck <= rq + (width - 1) * (tq // CHUNK)
            s1 = jnp.where(mask, s1, -jnp.inf)
            s2 = jnp.where(mask, s2, -jnp.inf)
        update(s1, vb1, m1_ref, a1_ref)
        update(s2, vb1, m2_ref, a2_ref)

    def wide_block(p, carry):
        block(ATTN_WIDE * p, ATTN_WIDE, False)
        return carry

    lax.fori_loop(0, i // ATTN_WIDE, wide_block, 0)
    for rem in range(ATTN_WIDE):
        @pl.when(i % ATTN_WIDE == rem)
        def _(rem=rem):
            block(i - rem, rem + 1, True)


    lam = lam_ref[0]
    o = a1_ref[:, :LANES] / a1_ref[:, LANES:] - lam * (a2_ref[:, :LANES] / a2_ref[:, LANES:])
    o = o * lax.rsqrt(jnp.mean(o * o, axis=-1, keepdims=True) + LN_EPS) * g_ref[...]
    o_ref[0] = (o * out_scale).astype(o_ref.dtype)


def _attention(lam, q, k, v, g, lambda_init, tq):
    b, s, _ = q.shape
    kern = functools.partial(_attn_kernel, tq=tq, out_scale=1.0 - lambda_init)
    kv_spec = pl.BlockSpec((1, s, 2 * A_HEAD_DIM), lambda bi, h, i: (bi, 0, h))
    qo_spec = pl.BlockSpec((1, tq, 2 * A_HEAD_DIM), lambda bi, h, i: (bi, i, h))
    vec = lambda n: pltpu.VMEM((tq, n), F32)
    return pl.pallas_call(
        kern,
        grid=(b, A_HEADS, s // tq),
        in_specs=[pl.BlockSpec(memory_space=pltpu.SMEM), qo_spec, kv_spec, kv_spec,
                  pl.BlockSpec((1, 2 * A_HEAD_DIM), lambda bi, h, i: (0, 0))],
        out_specs=qo_spec,
        out_shape=jax.ShapeDtypeStruct((b, s, D_MODEL), BF16),
        scratch_shapes=[vec(LANES), vec(2 * LANES)] * 2,
        compiler_params=_cparams(("arbitrary", "arbitrary", "arbitrary")),
        name="diff_attention",
    )(lam, q, k, v, g)


def _out_proj_kernel(o_ref, x_ref, w_ref, g_ref, b_ref, y_ref):
    h = jnp.dot(o_ref[...], w_ref[...], preferred_element_type=F32)
    y_ref[...] = _layer_norm(ALPHA * x_ref[...] + h, g_ref[...], b_ref[...])


def _out_proj(o2, x2, w, g, b, tm):
    t = x2.shape[0]
    row = lambda i: (i, 0)
    fixed = lambda i: (0, 0)
    return pl.pallas_call(
        _out_proj_kernel,
        grid=(t // tm,),
        in_specs=[pl.BlockSpec((tm, D_MODEL), row), pl.BlockSpec((tm, D_MODEL), row),
                  pl.BlockSpec((D_MODEL, D_MODEL), fixed),
                  pl.BlockSpec((1, D_MODEL), fixed), pl.BlockSpec((1, D_MODEL), fixed)],
        out_specs=pl.BlockSpec((tm, D_MODEL), row),
        out_shape=jax.ShapeDtypeStruct((t, D_MODEL), F32),
        compiler_params=_cparams(("arbitrary",)),
        name="out_proj_ln",
    )(o2, x2, w, g, b)


def _rotary_tables(seq):
    pos = jnp.arange(seq, dtype=F32)
    inv_freq = ROPE_THETA ** (-jnp.arange(0, ROT_DIM, 2, dtype=F32) / ROT_DIM)
    ang = pos[:, None] * inv_freq[None, :]
    cos, sin = jnp.cos(ang), jnp.sin(ang)
    half = ROT_DIM // 2
    pad = A_HEAD_DIM - ROT_DIM
    one = jnp.ones((seq, pad), F32)
    zero = jnp.zeros((seq, pad), F32)
    zh = jnp.zeros((seq, half), F32)
    cf = jnp.concatenate([cos, cos, one], axis=1)
    s1 = jnp.concatenate([-sin, zh, zero], axis=1)
    s2 = jnp.concatenate([zh, sin, zero], axis=1)
    rep = LANES // A_HEAD_DIM
    return tuple(jnp.tile(t, (1, rep)) for t in (cf, s1, s2))


def _attn_layer(x2, bsz, seq, w_in, w_out, lam_params, subln, lambda_init, ln_g, ln_b, tables,
                tm=512, tq=512):
    q, k, v = _attn_proj(x2, w_in.astype(BF16), *tables, seq, tm)
    lp = lam_params.astype(F32)
    lam = jnp.exp(jnp.sum(lp[0] * lp[1])) - jnp.exp(jnp.sum(lp[2] * lp[3])) + lambda_init
    shp = (bsz, seq, D_MODEL)
    o = _attention(lam.reshape(1), q.reshape(shp), k.reshape(shp), v.reshape(shp),
                   subln.reshape(1, -1).astype(F32), lambda_init, tq)
    return _out_proj(o.reshape(bsz * seq, D_MODEL), x2, w_out.astype(BF16),
                     ln_g.reshape(1, -1), ln_b.reshape(1, -1), tm)


H_CHUNK = 128
H_SUB = 16


def _hgrn_proj_kernel(x_ref, w_ref, lb_ref, q_ref, k_ref, v_ref, lf_ref, g_ref):
    xb = x_ref[...].astype(BF16)
    w = H_HEADS * H_EXPAND
    q = jnp.dot(xb, w_ref[:, 0:w], preferred_element_type=F32)
    q_ref[...] = q * jax.nn.sigmoid(q)
    f = jnp.dot(xb, w_ref[:, w:2 * w], preferred_element_type=F32)
    lb = lb_ref[...]
    forget = lb + (1.0 - lb) * jax.nn.sigmoid(f)
    lf_ref[...] = jnp.log(forget)
    k_ref[...] = 1.0 - forget
    v_ref[...] = jnp.dot(xb, w_ref[:, 2 * w:3 * w], preferred_element_type=F32)
    g = jnp.dot(xb, w_ref[:, 3 * w:4 * w], preferred_element_type=F32)
    g_ref[...] = g * jax.nn.sigmoid(g)


def _hgrn_proj(x2, w, lb, tm):
    t = x2.shape[0]
    row = lambda i: (i, 0)
    fixed = lambda i: (0, 0)
    out = jax.ShapeDtypeStruct((t, D_MODEL), F32)
    return pl.pallas_call(
        _hgrn_proj_kernel,
        grid=(t // tm,),
        in_specs=[pl.BlockSpec((tm, D_MODEL), row), pl.BlockSpec((D_MODEL, 4 * D_MODEL), fixed),
                  pl.BlockSpec((1, D_MODEL), fixed)],
        out_specs=[pl.BlockSpec((tm, D_MODEL), row)] * 5,
        out_shape=[out] * 5,
        compiler_params=_cparams(("arbitrary",)),
        name="hgrn_proj",
    )(x2, w, lb)


def _hgrn_chunk(q, k, v, lf, st):
    c, nb = H_CHUNK, H_CHUNK // H_SUB
    r_i = lax.broadcasted_iota(jnp.int32, (c, c), 0)
    c_i = lax.broadcasted_iota(jnp.int32, (c, c), 1)
    tri = (r_i >= c_i).astype(BF16)
    hi = lf.astype(BF16)
    lo = (lf - hi.astype(F32)).astype(BF16)
    b2 = jnp.dot(tri, jnp.concatenate([hi, lo], axis=1), preferred_element_type=F32)
    b = (b2[:, :H_EXPAND] + b2[:, H_EXPAND:]) * math.log2(math.e)
    b_last = b[c - 1:c, :]

    o = _dot_nt((q * jnp.exp2(b)).astype(BF16), st.astype(BF16))
    ke = k * jnp.exp2(b_last - b)
    st_new = st * jnp.exp2(b_last) + _dot_tn(v.astype(BF16), ke.astype(BF16))

    ends = [b[(j + 1) * H_SUB - 1:(j + 1) * H_SUB, :] for j in range(nb)]
    e_blk = jnp.concatenate([jnp.broadcast_to(e, (H_SUB, H_EXPAND)) for e in ends], axis=0)
    kt = k * jnp.exp2(e_blk - b)
    row_blk = lax.broadcasted_iota(jnp.int32, (c, H_EXPAND), 0) // H_SUB
    q_big = jnp.concatenate(
        [(q * jnp.exp2(jnp.minimum(b - ends[j], 0.0))).astype(BF16) for j in range(nb)], axis=1)
    k_big = jnp.concatenate(
        [jnp.where(row_blk == j, kt, 0.0).astype(BF16) for j in range(nb)], axis=1)
    a = _dot_nt(q_big, k_big)
    a = jnp.where(r_i // H_SUB > c_i // H_SUB, a, 0.0)
    o = o + jnp.dot(a.astype(BF16), v.astype(BF16), preferred_element_type=F32)

    t_idx = lax.broadcasted_iota(jnp.int32, (H_SUB, 1), 0)
    diag = []
    for j in range(nb):
        sl = slice(j * H_SUB, (j + 1) * H_SUB)
        qb, kb, vb, bb = q[sl], k[sl], v[sl], b[sl]
        od = jnp.zeros((H_SUB, H_EXPAND), F32)
        for s in range(H_SUB):
            e = jnp.exp2(bb - bb[s:s + 1, :])
            col = jnp.sum(qb * (kb[s:s + 1, :] * e), axis=-1, keepdims=True)
            od = od + jnp.where(t_idx >= s, col, 0.0) * vb[s:s + 1, :]
        diag.append(od)
    return o + jnp.concatenate(diag, axis=0), st_new


def _hgrn_kernel(q_ref, k_ref, v_ref, lf_ref, g_ref, ng_ref, o_ref, st_ref, *, n_chunks):
    @pl.when(pl.program_id(2) == 0)
    def _():
        st_ref[...] = jnp.zeros(st_ref.shape, F32)

    st = st_ref[...]
    for ci in range(n_chunks):
        sl = slice(ci * H_CHUNK, (ci + 1) * H_CHUNK)
        o, st = _hgrn_chunk(q_ref[sl, :], k_ref[sl, :], v_ref[sl, :], lf_ref[sl, :], st)
        o = o * lax.rsqrt(jnp.mean(o * o, axis=-1, keepdims=True) + LN_EPS) * ng_ref[...]
        o_ref[sl, :] = (o * g_ref[sl, :]).astype(o_ref.dtype)
    st_ref[...] = st


def _hgrn_recurrence(q, k, v, lf, g, ng, bsz, seq, tc):
    t = q.shape[0]
    nt = seq // tc
    blk = pl.BlockSpec((tc, H_EXPAND), lambda b, h, c: (b * nt + c, h))
    return pl.pallas_call(
        functools.partial(_hgrn_kernel, n_chunks=tc // H_CHUNK),
        grid=(bsz, H_HEADS, nt),
        in_specs=[blk] * 5 + [pl.BlockSpec((1, H_EXPAND), lambda b, h, c: (0, 0))],
        out_specs=blk,
        out_shape=jax.ShapeDtypeStruct((t, D_MODEL), BF16),
        scratch_shapes=[pltpu.VMEM((H_EXPAND, H_EXPAND), F32)],
        compiler_params=_cparams(("arbitrary", "arbitrary", "arbitrary")),
        name="hgrn_recurrence",
    )(q, k, v, lf, g, ng)


def _hgrn_layer(x2, bsz, seq, w_in, w_out, lb, norm_g, ln_g, ln_b, tm=256, tc=512):
    q, k, v, lf, g = _hgrn_proj(x2, w_in.astype(BF16), lb.reshape(1, -1).astype(F32), tm)
    o = _hgrn_recurrence(q, k, v, lf, g, norm_g.reshape(1, -1).astype(F32), bsz, seq, tc)
    return _out_proj(o, x2, w_out.astype(BF16), ln_g.reshape(1, -1), ln_b.reshape(1, -1), 2 * tm)


GROUP_SIZE = N_EXPERTS // N_GROUPS
X_WORDS = D_MODEL // 2
X_SLAB = X_WORDS // LANES


def _router_kernel(x_ref, wt_ref, bias_ref, idx_ref, rank_ref, gate_tm_ref, cnt_ref,
                   carry_ref, *, tr):
    @pl.when(pl.program_id(0) == 0)
    def _():
        carry_ref[...] = jnp.zeros(carry_ref.shape, F32)

    e_n = N_EXPERTS
    logits = lax.dot_general(wt_ref[...], x_ref[...], (((1,), (1,)), ((), ())),
                             precision=lax.Precision.HIGHEST, preferred_element_type=F32)
    scores = jax.nn.sigmoid(logits)
    choice = scores + bias_ref[...]

    ch3 = choice.reshape(N_GROUPS, GROUP_SIZE, tr)
    sub = lax.broadcasted_iota(jnp.int32, ch3.shape, 1)
    m1 = jnp.max(ch3, axis=1, keepdims=True)
    first = jnp.min(jnp.where(ch3 == m1, sub, GROUP_SIZE), axis=1, keepdims=True)
    m2 = jnp.max(jnp.where(sub == first, -jnp.inf, ch3), axis=1, keepdims=True)
    gs = (m1 + m2).reshape(N_GROUPS, tr)

    g_i = lax.broadcasted_iota(jnp.int32, gs.shape, 0)
    g_rank = jnp.zeros(gs.shape, F32)
    for g in range(N_GROUPS):
        row = gs[g:g + 1, :]
        ahead = (row > gs) | ((row == gs) & (g_i > g))
        g_rank = g_rank + jnp.where(ahead, 1.0, 0.0)
    g_keep = (g_rank < TOPK_GROUP).astype(F32).reshape(N_GROUPS, 1, tr)
    keep = jnp.broadcast_to(g_keep, (N_GROUPS, GROUP_SIZE, tr)).reshape(e_n, tr) > 0.5
    cm = jnp.where(keep, choice, -jnp.inf)

    e_i = lax.broadcasted_iota(jnp.int32, cm.shape, 0)
    sub_i = lax.broadcasted_iota(jnp.int32, (GROUP_SIZE, tr), 0)
    cm_g = [cm[g * GROUP_SIZE:(g + 1) * GROUP_SIZE, :] for g in range(N_GROUPS)]
    ranks = [jnp.zeros((GROUP_SIZE, tr), F32) for _ in range(N_GROUPS)]
    for e in range(e_n):
        ge, se = divmod(e, GROUP_SIZE)
        row = jnp.broadcast_to(cm_g[ge][se:se + 1, :], (GROUP_SIZE, tr))
        for g in range(N_GROUPS):
            if g > ge:
                ahead = row >= cm_g[g]
            elif g < ge:
                ahead = row > cm_g[g]
            else:
                ahead = (row > cm_g[g]) | ((row == cm_g[g]) & (sub_i > se))
            ranks[g] = ranks[g] + jnp.where(ahead, 1.0, 0.0)
    sel = jnp.concatenate(ranks, axis=0) < TOP_K
    sel_f = sel.astype(F32)
    w = jnp.where(sel, scores, 0.0)
    gate = w / (jnp.sum(w, axis=0, keepdims=True) + 1e-20) * ROUTED_SCALE

    sel_b = sel_f.astype(BF16)
    t_r = lax.broadcasted_iota(jnp.int32, (tr, tr), 0)
    t_c = lax.broadcasted_iota(jnp.int32, (tr, tr), 1)
    before = jnp.dot(sel_b, (t_r < t_c).astype(BF16), preferred_element_type=F32)
    tok_rank = carry_ref[:, 0:1] + before
    carry_ref[...] = carry_ref[...] + jnp.sum(sel_f, axis=1, keepdims=True)
    cnt_ref[...] = carry_ref[...]

    x_r = lax.broadcasted_iota(jnp.int32, (e_n, e_n), 0)
    x_c = lax.broadcasted_iota(jnp.int32, (e_n, e_n), 1)
    slot = jnp.dot((x_c < x_r).astype(BF16), sel_b, preferred_element_type=F32)
    e_f = e_i.astype(F32)
    idx_rows, gate_rows, rank_rows = [], [], []
    for j in range(TOP_K):
        pick = jnp.where(sel & (slot == j), 1.0, 0.0)
        idx_rows.append(jnp.sum(pick * e_f, axis=0, keepdims=True))
        gate_rows.append(jnp.sum(pick * gate, axis=0, keepdims=True))
        rank_rows.append(jnp.sum(pick * tok_rank, axis=0, keepdims=True))
    idx_ref[...] = jnp.concatenate(idx_rows, axis=0).astype(jnp.int32)
    rank_ref[...] = jnp.concatenate(rank_rows, axis=0).astype(jnp.int32)
    gates = jnp.concatenate(gate_rows, axis=0)
    padded = jnp.concatenate([gates, jnp.zeros((LANES - TOP_K, tr), F32)], axis=0)
    gate_tm_ref[...] = padded.T


def _router(x2, wt, bias, tr):
    t = x2.shape[0]
    col = lambda i: (0, i)
    return pl.pallas_call(
        functools.partial(_router_kernel, tr=tr),
        grid=(t // tr,),
        in_specs=[pl.BlockSpec((tr, D_MODEL), lambda i: (i, 0)),
                  pl.BlockSpec((N_EXPERTS, D_MODEL), lambda i: (0, 0)),
                  pl.BlockSpec((N_EXPERTS, 1), lambda i: (0, 0))],
        out_specs=[pl.BlockSpec((TOP_K, tr), col),
                   pl.BlockSpec((TOP_K, tr), col), pl.BlockSpec((tr, LANES), lambda i: (i, 0)),
                   pl.BlockSpec((N_EXPERTS, LANES), lambda i: (0, 0))],
        out_shape=[jax.ShapeDtypeStruct((TOP_K, t), jnp.int32),
                   jax.ShapeDtypeStruct((TOP_K, t), jnp.int32),
                   jax.ShapeDtypeStruct((t, LANES), F32),
                   jax.ShapeDtypeStruct((N_EXPERTS, LANES), F32)],
        scratch_shapes=[pltpu.VMEM((N_EXPERTS, LANES), F32)],
        compiler_params=_cparams(("arbitrary",)),
        name="moe_router",
    )(x2, wt, bias)


ISSUE_UNROLL = 8


def _pack_rows(x):
    lo = pltpu.bitcast(x[:, :X_WORDS].astype(BF16).astype(F32), jnp.uint32)
    hi = pltpu.bitcast(x[:, X_WORDS:].astype(BF16).astype(F32), jnp.uint32)
    return (lo >> 16) | (hi & jnp.uint32(0xFFFF0000))


def _unpack_words(words):
    return (pltpu.bitcast(words << 16, F32), pltpu.bitcast(words & jnp.uint32(0xFFFF0000), F32))


def _slots_kernel(start_ref, idx_ref, rank_ref, dest_ref):
    idx = idx_ref[...]
    dest = rank_ref[...]
    for e in range(N_EXPERTS):
        dest = dest + jnp.where(idx == e, start_ref[e], 0)
    dest_ref[...] = dest


def _slots(pad_start, idx, rank, tl):
    t = idx.shape[1]
    col = pl.BlockSpec((TOP_K, tl), lambda i: (0, i))
    return pl.pallas_call(
        _slots_kernel,
        grid=(t // tl,),
        in_specs=[pl.BlockSpec(memory_space=pltpu.SMEM), col, col],
        out_specs=col,
        out_shape=jax.ShapeDtypeStruct(idx.shape, jnp.int32),
        compiler_params=_cparams(("arbitrary",)),
        name="moe_slots",
    )(pad_start, idx, rank)


def _dispatch_kernel(dest_ref, x_ref, zeros_ref, xs_ref, xp_ref, sem, *, td):
    del zeros_ref
    words = _pack_rows(x_ref[...])
    for j in range(X_SLAB):
        xp_ref[:, j, :] = words[:, j * LANES:(j + 1) * LANES]

    def issue(g, carry):
        for u in range(ISSUE_UNROLL):
            t = g * ISSUE_UNROLL + u
            for j in range(TOP_K):
                pltpu.make_async_copy(xp_ref.at[t], xs_ref.at[dest_ref[j, t]], sem).start(priority=j % 2)
        return carry

    lax.fori_loop(0, td // ISSUE_UNROLL, issue, 0)
    for j in range(TOP_K):
        pltpu.make_async_copy(xp_ref, xs_ref.at[pl.ds(0, td)], sem).wait()


def _dispatch(dest, x2, zeros, td):
    t = x2.shape[0]
    return pl.pallas_call(
        functools.partial(_dispatch_kernel, td=td),
        grid=(t // td,),
        in_specs=[pl.BlockSpec((TOP_K, td), lambda i: (0, i), memory_space=pltpu.SMEM),
                  pl.BlockSpec((td, D_MODEL), lambda i: (i, 0)),
                  pl.BlockSpec(memory_space=pl.ANY)],
        out_specs=pl.BlockSpec(memory_space=pl.ANY),
        out_shape=jax.ShapeDtypeStruct(zeros.shape, zeros.dtype),
        input_output_aliases={2: 0},
        scratch_shapes=[pltpu.VMEM((td, X_SLAB, LANES), jnp.uint32), pltpu.SemaphoreType.DMA],
        compiler_params=_cparams(("arbitrary",)),
        name="moe_dispatch",
    )(dest, x2, zeros)


def _unpack_rows(words):
    return jnp.concatenate(_unpack_words(words), axis=1).astype(BF16)


def _swiglu(xb, w_gu, w_d):
    h = jnp.dot(xb, w_gu, preferred_element_type=F32)
    act = h[:, :EXPERT_FF] * jax.nn.sigmoid(h[:, :EXPERT_FF]) * h[:, EXPERT_FF:]
    return jnp.dot(act.astype(BF16), w_d, preferred_element_type=F32)


def _expert_kernel(blk_ref, exp_ref, used_ref, xs_ref, wgu_ref, wd_ref, ys_ref, wgu_b, wd_b, *, te):
    del blk_ref
    b = pl.program_id(0)
    live = b < used_ref[0]

    @pl.when(jnp.logical_or(b == 0, exp_ref[b] != exp_ref[jnp.maximum(b - 1, 0)]))
    def _():
        wgu_b[...] = wgu_ref[0, 0].astype(BF16)
        wd_b[...] = wd_ref[0, 0].astype(BF16)

    @pl.when(live)
    def _():
        words = jnp.concatenate(
            [xs_ref[pl.ds(j, te, stride=X_SLAB), :] for j in range(X_SLAB)], axis=1)
        y = _pack_rows(_swiglu(_unpack_rows(words), wgu_b[...], wd_b[...]))
        for c in range(X_SLAB):
            ys_ref[pl.ds(c, te, stride=X_SLAB), :] = y[:, c * LANES:(c + 1) * LANES]

    @pl.when(jnp.logical_not(live))
    def _():
        ys_ref[...] = jnp.zeros(ys_ref.shape, ys_ref.dtype)


def _experts(blk, exp, used, xs2, w_gu, w_d, layer, te):
    n_blocks = blk.shape[0]
    n_rows = xs2.shape[0] // X_SLAB
    return pl.pallas_call(
        functools.partial(_expert_kernel, te=te),
        grid_spec=pltpu.PrefetchScalarGridSpec(
            num_scalar_prefetch=3,
            grid=(n_blocks,),
            in_specs=[pl.BlockSpec((te * X_SLAB, LANES), lambda b, blk, exp, used: (blk[b], 0)),
                      pl.BlockSpec((1, 1, D_MODEL, 2 * EXPERT_FF),
                                   lambda b, blk, exp, used: (layer, exp[b], 0, 0)),
                      pl.BlockSpec((1, 1, EXPERT_FF, D_MODEL),
                                   lambda b, blk, exp, used: (layer, exp[b], 0, 0))],
            out_specs=pl.BlockSpec((te * X_SLAB, LANES), lambda b, blk, exp, used: (b, 0)),
            scratch_shapes=[pltpu.VMEM((D_MODEL, 2 * EXPERT_FF), BF16),
                            pltpu.VMEM((EXPERT_FF, D_MODEL), BF16)],
        ),
        out_shape=jax.ShapeDtypeStruct((n_rows * X_SLAB, LANES), jnp.uint32),
        compiler_params=_cparams(("arbitrary",)),
        name="moe_experts",
    )(blk, exp, used, xs2, w_gu, w_d)


def _combine_kernel(dest_ref, dest_next_ref, gate_ref, x_ref, wgu_ref, wd_ref, g_ref, b_ref, ys_ref, ys2_ref,
                    o_ref, ybuf_ref, sem, *, tc):
    i = pl.program_id(0)
    last = pl.num_programs(0) - 1
    slot = i % 2
    rows = TOP_K * tc

    def row_copy(d_ref, j, t, into):
        dst = ybuf_ref.at[pl.ds((into * rows + j * tc + t) * X_SLAB, X_SLAB)]
        return pltpu.make_async_copy(ys_ref.at[d_ref[j, t]], dst, sem.at[into])

    def wait_tile(into):
        pltpu.make_async_copy(ys2_ref.at[pl.ds(0, rows * X_SLAB)],
                              ybuf_ref.at[pl.ds(into * rows * X_SLAB, rows * X_SLAB)], sem.at[into]).wait()

    @pl.when(i == 0)
    def _():
        def issue(t, carry):
            for j in range(TOP_K):
                row_copy(dest_ref, j, t, 0).start(priority=j % 2)
            return carry
        lax.fori_loop(0, tc, issue, 0)

    wait_tile(slot)
    x = x_ref[...]
    gate = gate_ref[...]
    acc_lo = [None] * X_SLAB
    acc_hi = [None] * X_SLAB
    per_phase = tc // TOP_K
    for j in range(TOP_K):
        for t in range(j * per_phase, (j + 1) * per_phase):
            for jj in range(TOP_K):
                row_copy(dest_next_ref, jj, t, 1 - slot).start(priority=jj % 2)
        g_j = gate[:, j:j + 1]
        for c in range(X_SLAB):
            start = (slot * rows + j * tc) * X_SLAB + c
            lo, hi = _unpack_words(ybuf_ref[pl.ds(start, tc, stride=X_SLAB), :])
            acc_lo[c] = g_j * lo if j == 0 else acc_lo[c] + g_j * lo
            acc_hi[c] = g_j * hi if j == 0 else acc_hi[c] + g_j * hi
    routed = jnp.concatenate(acc_lo + acc_hi, axis=1)
    acc = ALPHA * x + routed + _swiglu(x.astype(BF16), wgu_ref[...], wd_ref[...])
    o_ref[...] = _layer_norm(acc, g_ref[...], b_ref[...])

    @pl.when(i == last)
    def _():
        wait_tile(1 - slot)


def _combine(dest, gate_tm, x2, ys, ws_gu, ws_d, g, b, tc):
    t = x2.shape[0]
    n = t // tc
    fixed = lambda i: (0, 0)
    row = lambda i: (i, 0)
    smem = lambda f: pl.BlockSpec((TOP_K, tc), f, memory_space=pltpu.SMEM)
    return pl.pallas_call(
        functools.partial(_combine_kernel, tc=tc),
        grid=(n,),
        in_specs=[smem(lambda i: (0, i)), smem(lambda i: (0, jnp.minimum(i + 1, n - 1))),
                  pl.BlockSpec((tc, LANES), row), pl.BlockSpec((tc, D_MODEL), row),
                  pl.BlockSpec((D_MODEL, 2 * EXPERT_FF), fixed), pl.BlockSpec((EXPERT_FF, D_MODEL), fixed),
                  pl.BlockSpec((1, D_MODEL), fixed), pl.BlockSpec((1, D_MODEL), fixed),
                  pl.BlockSpec(memory_space=pl.ANY), pl.BlockSpec(memory_space=pl.ANY)],
        out_specs=pl.BlockSpec((tc, D_MODEL), row),
        out_shape=jax.ShapeDtypeStruct((t, D_MODEL), F32),
        scratch_shapes=[pltpu.VMEM((2 * TOP_K * tc * X_SLAB, LANES), jnp.uint32),
                        pltpu.SemaphoreType.DMA((2,))],
        compiler_params=_cparams(("arbitrary",)),
        name="moe_combine",
    )(dest, dest, gate_tm, x2, ws_gu, ws_d, g, b, ys.reshape(-1, X_SLAB, LANES), ys)


def _moe_layer(x2, w_router, bias, w_gu, w_d, layer, ws_gu, ws_d, ln_g, ln_b, sorted_buf=None,
               te=512, tr=512, td=512, tc=128):
    t = x2.shape[0]
    n_blocks = t * TOP_K // te + N_EXPERTS
    n_rows = n_blocks * te
    if sorted_buf is None:
        sorted_buf = jnp.zeros((n_rows, X_SLAB, LANES), jnp.uint32)
    idx, rank, gate_tm, cnt = _router(x2, w_router.T.astype(F32), bias.reshape(-1, 1).astype(F32), tr)

    counts = cnt[:, 0].astype(jnp.int32)
    padded = (counts + te - 1) // te * te
    pad_end = jnp.cumsum(padded)
    pad_start = pad_end - padded
    dest = _slots(pad_start, idx, rank, min(t, 4096))
    used = pad_end[-1] // te
    blk = jnp.minimum(jnp.arange(n_blocks, dtype=jnp.int32), used - 1)
    exp = jnp.sum((pad_end[None, :] // te <= blk[:, None]).astype(jnp.int32), axis=1)
    exp = jnp.minimum(exp, N_EXPERTS - 1)

    xs = _dispatch(dest, x2, sorted_buf, td)
    ys = _experts(blk, exp.astype(jnp.int32), used.reshape(1), xs.reshape(n_rows * X_SLAB, LANES),
                  w_gu, w_d, layer, te)
    out = _combine(dest, gate_tm, x2, ys, ws_gu.astype(BF16), ws_d.astype(BF16),
                   ln_g.reshape(1, -1), ln_b.reshape(1, -1), tc)
    return out, xs


def kernel(x, attn_w_in, attn_w_out, attn_lambda, attn_subln, hgrn_w_in, hgrn_w_out, hgrn_lower_bounds, hgrn_norm, moe_router, moe_router_bias, moe_w_gate_up, moe_w_down, shared_w_gate_up, shared_w_down, ln_gain, ln_bias):
    bsz, seq, _ = x.shape
    tables = _rotary_tables(seq)
    lb = jax.nn.softmax(hgrn_lower_bounds.astype(F32), axis=0)
    lb = jnp.cumsum(lb, axis=0) - lb[0]
    x2 = x.reshape(bsz * seq, D_MODEL)
    sorted_buf = None
    for layer in range(DEPTH):
        j = layer // 2
        if layer % 2 == 0:
            lambda_init = 0.8 - 0.6 * math.exp(-0.3 * layer)
            x2 = _attn_layer(x2, bsz, seq, attn_w_in[j], attn_w_out[j], attn_lambda[j], attn_subln[j],
                             lambda_init, ln_gain[layer, 0], ln_bias[layer, 0], tables)
        else:
            x2 = _hgrn_layer(x2, bsz, seq, hgrn_w_in[j], hgrn_w_out[j], lb[layer], hgrn_norm[j],
                             ln_gain[layer, 0], ln_bias[layer, 0])
        x2, sorted_buf = _moe_layer(x2, moe_router[layer], moe_router_bias[layer], moe_w_gate_up,
                                    moe_w_down, layer, shared_w_gate_up[layer], shared_w_down[layer],
                                    ln_gain[layer, 1], ln_bias[layer, 1], sorted_buf)
    return x2.reshape(bsz, seq, D_MODEL)
```

```python
import functools
import math

import jax
import jax.numpy as jnp
from jax import lax
from jax.experimental import pallas as pl
from jax.experimental.pallas import tpu as pltpu

F32 = jnp.float32
BF16 = jnp.bfloat16

D_MODEL = 1024
DEPTH = 4
CHUNK = 64
A_HEADS = 8
A_HEAD_DIM = 64
ROT_DIM = 16
ROPE_THETA = 500000.0
H_HEADS = 8
H_EXPAND = 128
N_EXPERTS = 64
N_GROUPS = 8
TOPK_GROUP = 4
TOP_K = 8
EXPERT_FF = 256
ROUTED_SCALE = 2.5
ALPHA = (2 * DEPTH) ** 0.25
LN_EPS = 1e-5
LANES = 128
VMEM_LIMIT = 48 * 1024 * 1024


def _cparams(sem):
    return pltpu.CompilerParams(dimension_semantics=sem, vmem_limit_bytes=VMEM_LIMIT)


def _layer_norm(y, g, b):
    mu = jnp.mean(y, axis=-1, keepdims=True)
    yc = y - mu
    var = jnp.mean(yc * yc, axis=-1, keepdims=True)
    return yc * lax.rsqrt(var + LN_EPS) * g + b


def _dot_nt(a, b):
    return lax.dot_general(a, b, (((1,), (1,)), ((), ())), preferred_element_type=F32)


def _dot_tn(a, b):
    return lax.dot_general(a, b, (((0,), (0,)), ((), ())), preferred_element_type=F32)


def _attn_proj_kernel(x_ref, w_ref, cf_ref, s1_ref, s2_ref, q_ref, k_ref, v_ref):
    xb = x_ref[...].astype(BF16)
    cf, s1, s2 = cf_ref[...], s1_ref[...], s2_ref[...]

    def rotary(y):
        outs = []
        for c in range(D_MODEL // LANES):
            yc = y[:, c * LANES:(c + 1) * LANES]
            outs.append(yc * cf + pltpu.roll(yc, LANES - ROT_DIM // 2, 1) * s1
                        + pltpu.roll(yc, ROT_DIM // 2, 1) * s2)
        return jnp.concatenate(outs, axis=1)

    q = jnp.dot(xb, w_ref[:, 0:D_MODEL], preferred_element_type=F32)
    q_ref[...] = (rotary(q) * (A_HEAD_DIM ** -0.5 * math.log2(math.e))).astype(BF16)
    k = jnp.dot(xb, w_ref[:, D_MODEL:2 * D_MODEL], preferred_element_type=F32)
    k_ref[...] = rotary(k).astype(BF16)
    v = jnp.dot(xb, w_ref[:, 2 * D_MODEL:3 * D_MODEL], preferred_element_type=F32)
    v_ref[...] = v.astype(BF16)


def _attn_proj(x2, w, cf, s1, s2, seq, tm):
    t = x2.shape[0]
    nseq = seq // tm
    row = lambda i: (i, 0)
    tab = lambda i: (i % nseq, 0)
    out = jax.ShapeDtypeStruct((t, D_MODEL), BF16)
    return pl.pallas_call(
        _attn_proj_kernel,
        grid=(t // tm,),
        in_specs=[pl.BlockSpec((tm, D_MODEL), row),
                  pl.BlockSpec((D_MODEL, 3 * D_MODEL), lambda i: (0, 0)),
                  pl.BlockSpec((tm, LANES), tab), pl.BlockSpec((tm, LANES), tab),
                  pl.BlockSpec((tm, LANES), tab)],
        out_specs=[pl.BlockSpec((tm, D_MODEL), row)] * 3,
        out_shape=[out, out, out],
        compiler_params=_cparams(("arbitrary",)),
        name="attn_proj",
    )(x2, w, cf, s1, s2)


ATTN_WIDE = 4


def _attn_kernel(lam_ref, q_ref, k_ref, v_ref, g_ref, o_ref, m1_ref, a1_ref, m2_ref, a2_ref,
                 *, tq, out_scale):
    i = pl.program_id(2)
    q = q_ref[0]
    lane = lax.broadcasted_iota(jnp.int32, q.shape, 1)
    zero = jnp.zeros_like(q)
    q1 = jnp.where(lane < A_HEAD_DIM, q, zero)
    q2 = jnp.where(lane >= A_HEAD_DIM, q, zero)
    q12 = jnp.concatenate([q1, q2], axis=0)

    for m_ref, a_ref in ((m1_ref, a1_ref), (m2_ref, a2_ref)):
        m_ref[...] = jnp.full(m_ref.shape, -jnp.inf, F32)
        a_ref[...] = jnp.zeros(a_ref.shape, F32)

    def update(s, vb1, m_ref, a_ref):
        tiles = [s[:, c * LANES:(c + 1) * LANES] for c in range(s.shape[1] // LANES)]
        mc = functools.reduce(jnp.maximum, tiles)
        m_old = m_ref[...]
        m_new = jnp.maximum(m_old, jnp.max(mc, axis=-1, keepdims=True))
        alpha = jnp.exp2(m_old - m_new)
        pb = jnp.concatenate([jnp.exp2((t - m_new).astype(BF16)) for t in tiles], axis=1)
        pv = jnp.dot(pb, vb1, preferred_element_type=F32)
        a_ref[:, :LANES] = alpha * a_ref[:, :LANES] + pv[:, :LANES]
        a_ref[:, LANES:] = alpha * a_ref[:, LANES:] + pv[:, LANES:]
        m_ref[...] = m_new

    def block(j, width, diagonal):
        start = pl.multiple_of(j * tq, tq)
        kb = k_ref[0, pl.ds(start, width * tq), :]
        vb1 = jnp.concatenate([v_ref[0, pl.ds(start, width * tq), :],
                               jnp.ones((width * tq, LANES), BF16)], axis=1)
        s = _dot_nt(q12, kb)
        s1, s2 = s[:tq], s[tq:]
        if diagonal:
            rq = lax.broadcasted_iota(jnp.int32, (tq, width * tq), 0) // CHUNK
            ck = lax.broadcasted_iota(jnp.int32, (tq, width * tq), 1) // CHUNK
            mask = ck <= rq + (width - 1) * (tq // CHUNK)
            s1 = jnp.where(mask, s1, -jnp.inf)
            s2 = jnp.where(mask, s2, -jnp.inf)
        update(s1, vb1, m1_ref, a1_ref)
        update(s2, vb1, m2_ref, a2_ref)

    def wide_block(p, carry):
        block(ATTN_WIDE * p, ATTN_WIDE, False)
        return carry

    lax.fori_loop(0, i // ATTN_WIDE, wide_block, 0)
    for rem in range(ATTN_WIDE):
        @pl.when(i % ATTN_WIDE == rem)
        def _(rem=rem):
            block(i - rem, rem + 1, True)


    lam = lam_ref[0]
    o = a1_ref[:, :LANES] / a1_ref[:, LANES:] - lam * (a2_ref[:, :LANES] / a2_ref[:, LANES:])
    o = o * lax.rsqrt(jnp.mean(o * o, axis=-1, keepdims=True) + LN_EPS) * g_ref[...]
    o_ref[0] = (o * out_scale).astype(o_ref.dtype)


def _attention(lam, q, k, v, g, lambda_init, tq):
    b, s, _ = q.shape
    kern = functools.partial(_attn_kernel, tq=tq, out_scale=1.0 - lambda_init)
    kv_spec = pl.BlockSpec((1, s, 2 * A_HEAD_DIM), lambda bi, h, i: (bi, 0, h))
    qo_spec = pl.BlockSpec((1, tq, 2 * A_HEAD_DIM), lambda bi, h, i: (bi, i, h))
    vec = lambda n: pltpu.VMEM((tq, n), F32)
    return pl.pallas_call(
        kern,
        grid=(b, A_HEADS, s // tq),
        in_specs=[pl.BlockSpec(memory_space=pltpu.SMEM), qo_spec, kv_spec, kv_spec,
                  pl.BlockSpec((1, 2 * A_HEAD_DIM), lambda bi, h, i: (0, 0))],
        out_specs=qo_spec,
        out_shape=jax.ShapeDtypeStruct((b, s, D_MODEL), BF16),
        scratch_shapes=[vec(LANES), vec(2 * LANES)] * 2,
        compiler_params=_cparams(("arbitrary", "arbitrary", "arbitrary")),
        name="diff_attention",
    )(lam, q, k, v, g)


def _out_proj_kernel(o_ref, x_ref, w_ref, g_ref, b_ref, y_ref):
    h = jnp.dot(o_ref[...], w_ref[...], preferred_element_type=F32)
    y_ref[...] = _layer_norm(ALPHA * x_ref[...] + h, g_ref[...], b_ref[...])


def _out_proj(o2, x2, w, g, b, tm):
    t = x2.shape[0]
    row = lambda i: (i, 0)
    fixed = lambda i: (0, 0)
    return pl.pallas_call(
        _out_proj_kernel,
        grid=(t // tm,),
        in_specs=[pl.BlockSpec((tm, D_MODEL), row), pl.BlockSpec((tm, D_MODEL), row),
                  pl.BlockSpec((D_MODEL, D_MODEL), fixed),
                  pl.BlockSpec((1, D_MODEL), fixed), pl.BlockSpec((1, D_MODEL), fixed)],
        out_specs=pl.BlockSpec((tm, D_MODEL), row),
        out_shape=jax.ShapeDtypeStruct((t, D_MODEL), F32),
        compiler_params=_cparams(("arbitrary",)),
        name="out_proj_ln",
    )(o2, x2, w, g, b)


def _rotary_tables(seq):
    pos = jnp.arange(seq, dtype=F32)
    inv_freq = ROPE_THETA ** (-jnp.arange(0, ROT_DIM, 2, dtype=F32) / ROT_DIM)
    ang = pos[:, None] * inv_freq[None, :]
    cos, sin = jnp.cos(ang), jnp.sin(ang)
    half = ROT_DIM // 2
    pad = A_HEAD_DIM - ROT_DIM
    one = jnp.ones((seq, pad), F32)
    zero = jnp.zeros((seq, pad), F32)
    zh = jnp.zeros((seq, half), F32)
    cf = jnp.concatenate([cos, cos, one], axis=1)
    s1 = jnp.concatenate([-sin, zh, zero], axis=1)
    s2 = jnp.concatenate([zh, sin, zero], axis=1)
    rep = LANES // A_HEAD_DIM
    return tuple(jnp.tile(t, (1, rep)) for t in (cf, s1, s2))


def _attn_layer(x2, bsz, seq, w_in, w_out, lam_params, subln, lambda_init, ln_g, ln_b, tables,
                tm=512, tq=512):
    q, k, v = _attn_proj(x2, w_in.astype(BF16), *tables, seq, tm)
    lp = lam_params.astype(F32)
    lam = jnp.exp(jnp.sum(lp[0] * lp[1])) - jnp.exp(jnp.sum(lp[2] * lp[3])) + lambda_init
    shp = (bsz, seq, D_MODEL)
    o = _attention(lam.reshape(1), q.reshape(shp), k.reshape(shp), v.reshape(shp),
                   subln.reshape(1, -1).astype(F32), lambda_init, tq)
    return _out_proj(o.reshape(bsz * seq, D_MODEL), x2, w_out.astype(BF16),
                     ln_g.reshape(1, -1), ln_b.reshape(1, -1), tm)


H_CHUNK = 128
H_SUB = 16


def _hgrn_proj_kernel(x_ref, w_ref, lb_ref, q_ref, k_ref, v_ref, lf_ref, g_ref):
    xb = x_ref[...].astype(BF16)
    w = H_HEADS * H_EXPAND
    q = jnp.dot(xb, w_ref[:, 0:w], preferred_element_type=F32)
    q_ref[...] = q * jax.nn.sigmoid(q)
    f = jnp.dot(xb, w_ref[:, w:2 * w], preferred_element_type=F32)
    lb = lb_ref[...]
    forget = lb + (1.0 - lb) * jax.nn.sigmoid(f)
    lf_ref[...] = jnp.log(forget)
    k_ref[...] = 1.0 - forget
    v_ref[...] = jnp.dot(xb, w_ref[:, 2 * w:3 * w], preferred_element_type=F32)
    g = jnp.dot(xb, w_ref[:, 3 * w:4 * w], preferred_element_type=F32)
    g_ref[...] = g * jax.nn.sigmoid(g)


def _hgrn_proj(x2, w, lb, tm):
    t = x2.shape[0]
    row = lambda i: (i, 0)
    fixed = lambda i: (0, 0)
    out = jax.ShapeDtypeStruct((t, D_MODEL), F32)
    return pl.pallas_call(
        _hgrn_proj_kernel,
        grid=(t // tm,),
        in_specs=[pl.BlockSpec((tm, D_MODEL), row), pl.BlockSpec((D_MODEL, 4 * D_MODEL), fixed),
                  pl.BlockSpec((1, D_MODEL), fixed)],
        out_specs=[pl.BlockSpec((tm, D_MODEL), row)] * 5,
        out_shape=[out] * 5,
        compiler_params=_cparams(("arbitrary",)),
        name="hgrn_proj",
    )(x2, w, lb)


H_FAST_SPREAD = 96.0


def _hgrn_cumsum(lf):
    c = H_CHUNK
    r_i = lax.broadcasted_iota(jnp.int32, (c, c), 0)
    c_i = lax.broadcasted_iota(jnp.int32, (c, c), 1)
    tri = (r_i >= c_i).astype(BF16)
    hi = lf.astype(BF16)
    lo = (lf - hi.astype(F32)).astype(BF16)
    b2 = jnp.dot(tri, jnp.concatenate([hi, lo], axis=1), preferred_element_type=F32)
    return (b2[:, :H_EXPAND] + b2[:, H_EXPAND:]) * math.log2(math.e)


def _hgrn_spread(b):
    nb = H_CHUNK // H_SUB
    ends = [b[(j + 1) * H_SUB - 1:(j + 1) * H_SUB, :] for j in range(nb)]
    drops = [-ends[0]] + [ends[j - 1] - ends[j] for j in range(1, nb)]
    return functools.reduce(jnp.maximum, drops)


def _hgrn_chunk(q, k, v, b, st, factored):
    c, nb = H_CHUNK, H_CHUNK // H_SUB
    r_i = lax.broadcasted_iota(jnp.int32, (c, c), 0)
    c_i = lax.broadcasted_iota(jnp.int32, (c, c), 1)
    b_last = b[c - 1:c, :]

    o = _dot_nt((q * jnp.exp2(b)).astype(BF16), st.astype(BF16))
    ke = k * jnp.exp2(b_last - b)
    st_new = st * jnp.exp2(b_last) + _dot_tn(v.astype(BF16), ke.astype(BF16))

    ends = [b[(j + 1) * H_SUB - 1:(j + 1) * H_SUB, :] for j in range(nb)]
    e_blk = jnp.concatenate([jnp.broadcast_to(e, (H_SUB, H_EXPAND)) for e in ends], axis=0)
    kt = k * jnp.exp2(e_blk - b)
    row_blk = lax.broadcasted_iota(jnp.int32, (c, H_EXPAND), 0) // H_SUB
    q_big = jnp.concatenate(
        [(q * jnp.exp2(jnp.minimum(b - ends[j], 0.0))).astype(BF16) for j in range(nb)], axis=1)
    k_big = jnp.concatenate(
        [jnp.where(row_blk == j, kt, 0.0).astype(BF16) for j in range(nb)], axis=1)
    a = _dot_nt(q_big, k_big)
    a = jnp.where(r_i // H_SUB > c_i // H_SUB, a, 0.0)

    if factored:
        a_d = _dot_nt((q * jnp.exp2(b - e_blk)).astype(BF16), kt.astype(BF16))
        a = a + jnp.where((r_i // H_SUB == c_i // H_SUB) & (r_i >= c_i), a_d, 0.0)
        return o + jnp.dot(a.astype(BF16), v.astype(BF16), preferred_element_type=F32), st_new

    o = o + jnp.dot(a.astype(BF16), v.astype(BF16), preferred_element_type=F32)
    t_idx = lax.broadcasted_iota(jnp.int32, (H_SUB, 1), 0)
    diag = []
    for j in range(nb):
        sl = slice(j * H_SUB, (j + 1) * H_SUB)
        qb, kb, vb, bb = q[sl], k[sl], v[sl], b[sl]
        od = jnp.zeros((H_SUB, H_EXPAND), F32)
        for s in range(H_SUB):
            e = jnp.exp2(bb - bb[s:s + 1, :])
            col = jnp.sum(qb * (kb[s:s + 1, :] * e), axis=-1, keepdims=True)
            od = od + jnp.where(t_idx >= s, col, 0.0) * vb[s:s + 1, :]
        diag.append(od)
    return o + jnp.concatenate(diag, axis=0), st_new


def _hgrn_kernel(q_ref, k_ref, v_ref, lf_ref, g_ref, ng_ref, o_ref, st_ref, b_ref, *, n_chunks):
    @pl.when(pl.program_id(2) == 0)
    def _():
        st_ref[...] = jnp.zeros(st_ref.shape, F32)

    spread = None
    for ci in range(n_chunks):
        sl = slice(ci * H_CHUNK, (ci + 1) * H_CHUNK)
        b = _hgrn_cumsum(lf_ref[sl, :])
        b_ref[sl, :] = b
        sp = _hgrn_spread(b)
        spread = sp if spread is None else jnp.maximum(spread, sp)
    mild = jnp.max(spread) < H_FAST_SPREAD

    def run(factored):
        st = st_ref[...]
        for ci in range(n_chunks):
            sl = slice(ci * H_CHUNK, (ci + 1) * H_CHUNK)
            o, st = _hgrn_chunk(q_ref[sl, :], k_ref[sl, :], v_ref[sl, :], b_ref[sl, :], st, factored)
            o = o * lax.rsqrt(jnp.mean(o * o, axis=-1, keepdims=True) + LN_EPS) * ng_ref[...]
            o_ref[sl, :] = (o * g_ref[sl, :]).astype(o_ref.dtype)
        st_ref[...] = st

    @pl.when(mild)
    def _():
        run(True)

    @pl.when(jnp.logical_not(mild))
    def _():
        run(False)


def _hgrn_recurrence(q, k, v, lf, g, ng, bsz, seq, tc):
    t = q.shape[0]
    nt = seq // tc
    blk = pl.BlockSpec((tc, H_EXPAND), lambda b, h, c: (b * nt + c, h))
    return pl.pallas_call(
        functools.partial(_hgrn_kernel, n_chunks=tc // H_CHUNK),
        grid=(bsz, H_HEADS, nt),
        in_specs=[blk] * 5 + [pl.BlockSpec((1, H_EXPAND), lambda b, h, c: (0, 0))],
        out_specs=blk,
        out_shape=jax.ShapeDtypeStruct((t, D_MODEL), BF16),
        scratch_shapes=[pltpu.VMEM((H_EXPAND, H_EXPAND), F32), pltpu.VMEM((tc, H_EXPAND), F32)],
        compiler_params=_cparams(("arbitrary", "arbitrary", "arbitrary")),
        name="hgrn_recurrence",
    )(q, k, v, lf, g, ng)


def _hgrn_layer(x2, bsz, seq, w_in, w_out, lb, norm_g, ln_g, ln_b, tm=256, tc=1024):
    q, k, v, lf, g = _hgrn_proj(x2, w_in.astype(BF16), lb.reshape(1, -1).astype(F32), tm)
    o = _hgrn_recurrence(q, k, v, lf, g, norm_g.reshape(1, -1).astype(F32), bsz, seq, tc)
    return _out_proj(o, x2, w_out.astype(BF16), ln_g.reshape(1, -1), ln_b.reshape(1, -1), 2 * tm)


GROUP_SIZE = N_EXPERTS // N_GROUPS
X_WORDS = D_MODEL // 2
X_SLAB = X_WORDS // LANES


def _router_kernel(x_ref, wt_ref, bias_ref, idx_ref, rank_ref, gate_tm_ref, cnt_ref,
                   carry_ref, *, tr):
    @pl.when(pl.program_id(0) == 0)
    def _():
        carry_ref[...] = jnp.zeros(carry_ref.shape, F32)

    e_n = N_EXPERTS
    logits = lax.dot_general(wt_ref[...], x_ref[...], (((1,), (1,)), ((), ())),
                             precision=lax.Precision.HIGHEST, preferred_element_type=F32)
    scores = jax.nn.sigmoid(logits)
    choice = scores + bias_ref[...]

    ch3 = choice.reshape(N_GROUPS, GROUP_SIZE, tr)
    sub = lax.broadcasted_iota(jnp.int32, ch3.shape, 1)
    m1 = jnp.max(ch3, axis=1, keepdims=True)
    first = jnp.min(jnp.where(ch3 == m1, sub, GROUP_SIZE), axis=1, keepdims=True)
    m2 = jnp.max(jnp.where(sub == first, -jnp.inf, ch3), axis=1, keepdims=True)
    gs = (m1 + m2).reshape(N_GROUPS, tr)

    g_i = lax.broadcasted_iota(jnp.int32, gs.shape, 0)
    g_rank = jnp.zeros(gs.shape, F32)
    for g in range(N_GROUPS):
        row = gs[g:g + 1, :]
        ahead = (row > gs) | ((row == gs) & (g_i > g))
        g_rank = g_rank + jnp.where(ahead, 1.0, 0.0)
    g_keep = (g_rank < TOPK_GROUP).astype(F32).reshape(N_GROUPS, 1, tr)
    keep = jnp.broadcast_to(g_keep, (N_GROUPS, GROUP_SIZE, tr)).reshape(e_n, tr) > 0.5
    cm = jnp.where(keep, choice, -jnp.inf)

    e_i = lax.broadcasted_iota(jnp.int32, cm.shape, 0)
    sub_i = lax.broadcasted_iota(jnp.int32, (GROUP_SIZE, tr), 0)
    cm_g = [cm[g * GROUP_SIZE:(g + 1) * GROUP_SIZE, :] for g in range(N_GROUPS)]
    ranks = [jnp.zeros((GROUP_SIZE, tr), F32) for _ in range(N_GROUPS)]
    for e in range(e_n):
        ge, se = divmod(e, GROUP_SIZE)
        row = jnp.broadcast_to(cm_g[ge][se:se + 1, :], (GROUP_SIZE, tr))
        for g in range(N_GROUPS):
            if g > ge:
                ahead = row >= cm_g[g]
            elif g < ge:
                ahead = row > cm_g[g]
            else:
                ahead = (row > cm_g[g]) | ((row == cm_g[g]) & (sub_i > se))
            ranks[g] = ranks[g] + jnp.where(ahead, 1.0, 0.0)
    sel = jnp.concatenate(ranks, axis=0) < TOP_K
    sel_f = sel.astype(F32)
    w = jnp.where(sel, scores, 0.0)
    gate = w / (jnp.sum(w, axis=0, keepdims=True) + 1e-20) * ROUTED_SCALE

    sel_b = sel_f.astype(BF16)
    t_r = lax.broadcasted_iota(jnp.int32, (tr, tr), 0)
    t_c = lax.broadcasted_iota(jnp.int32, (tr, tr), 1)
    before = jnp.dot(sel_b, (t_r < t_c).astype(BF16), preferred_element_type=F32)
    tok_rank = carry_ref[:, 0:1] + before
    carry_ref[...] = carry_ref[...] + jnp.sum(sel_f, axis=1, keepdims=True)
    cnt_ref[...] = carry_ref[...]

    x_r = lax.broadcasted_iota(jnp.int32, (e_n, e_n), 0)
    x_c = lax.broadcasted_iota(jnp.int32, (e_n, e_n), 1)
    slot = jnp.dot((x_c < x_r).astype(BF16), sel_b, preferred_element_type=F32)
    e_f = e_i.astype(F32)
    idx_rows, gate_rows, rank_rows = [], [], []
    for j in range(TOP_K):
        pick = jnp.where(sel & (slot == j), 1.0, 0.0)
        idx_rows.append(jnp.sum(pick * e_f, axis=0, keepdims=True))
        gate_rows.append(jnp.sum(pick * gate, axis=0, keepdims=True))
        rank_rows.append(jnp.sum(pick * tok_rank, axis=0, keepdims=True))
    idx_ref[...] = jnp.concatenate(idx_rows, axis=0).astype(jnp.int32)
    rank_ref[...] = jnp.concatenate(rank_rows, axis=0).astype(jnp.int32)
    gates = jnp.concatenate(gate_rows, axis=0)
    padded = jnp.concatenate([gates, jnp.zeros((LANES - TOP_K, tr), F32)], axis=0)
    gate_tm_ref[...] = padded.T


def _router(x2, wt, bias, tr):
    t = x2.shape[0]
    col = lambda i: (0, i)
    return pl.pallas_call(
        functools.partial(_router_kernel, tr=tr),
        grid=(t // tr,),
        in_specs=[pl.BlockSpec((tr, D_MODEL), lambda i: (i, 0)),
                  pl.BlockSpec((N_EXPERTS, D_MODEL), lambda i: (0, 0)),
                  pl.BlockSpec((N_EXPERTS, 1), lambda i: (0, 0))],
        out_specs=[pl.BlockSpec((TOP_K, tr), col),
                   pl.BlockSpec((TOP_K, tr), col), pl.BlockSpec((tr, LANES), lambda i: (i, 0)),
                   pl.BlockSpec((N_EXPERTS, LANES), lambda i: (0, 0))],
        out_shape=[jax.ShapeDtypeStruct((TOP_K, t), jnp.int32),
                   jax.ShapeDtypeStruct((TOP_K, t), jnp.int32),
                   jax.ShapeDtypeStruct((t, LANES), F32),
                   jax.ShapeDtypeStruct((N_EXPERTS, LANES), F32)],
        scratch_shapes=[pltpu.VMEM((N_EXPERTS, LANES), F32)],
        compiler_params=_cparams(("arbitrary",)),
        name="moe_router",
    )(x2, wt, bias)


ISSUE_UNROLL = 8


def _pack_rows(x):
    lo = pltpu.bitcast(x[:, :X_WORDS].astype(BF16).astype(F32), jnp.uint32)
    hi = pltpu.bitcast(x[:, X_WORDS:].astype(BF16).astype(F32), jnp.uint32)
    return (lo >> 16) | (hi & jnp.uint32(0xFFFF0000))


def _unpack_words(words):
    return (pltpu.bitcast(words << 16, F32), pltpu.bitcast(words & jnp.uint32(0xFFFF0000), F32))


def _slots_kernel(start_ref, idx_ref, rank_ref, dest_ref):
    idx = idx_ref[...]
    dest = rank_ref[...]
    for e in range(N_EXPERTS):
        dest = dest + jnp.where(idx == e, start_ref[e], 0)
    dest_ref[...] = dest


def _slots(pad_start, idx, rank, tl):
    t = idx.shape[1]
    col = pl.BlockSpec((TOP_K, tl), lambda i: (0, i))
    return pl.pallas_call(
        _slots_kernel,
        grid=(t // tl,),
        in_specs=[pl.BlockSpec(memory_space=pltpu.SMEM), col, col],
        out_specs=col,
        out_shape=jax.ShapeDtypeStruct(idx.shape, jnp.int32),
        compiler_params=_cparams(("arbitrary",)),
        name="moe_slots",
    )(pad_start, idx, rank)


def _dispatch_kernel(dest_ref, x_ref, zeros_ref, xs_ref, xp_ref, sem, *, td):
    del zeros_ref
    words = _pack_rows(x_ref[...])
    for j in range(X_SLAB):
        xp_ref[:, j, :] = words[:, j * LANES:(j + 1) * LANES]

    def issue(g, carry):
        for u in range(ISSUE_UNROLL):
            t = g * ISSUE_UNROLL + u
            for j in range(TOP_K):
                pltpu.make_async_copy(xp_ref.at[t], xs_ref.at[dest_ref[j, t]], sem).start(priority=j % 2)
        return carry

    lax.fori_loop(0, td // ISSUE_UNROLL, issue, 0)
    for j in range(TOP_K):
        pltpu.make_async_copy(xp_ref, xs_ref.at[pl.ds(0, td)], sem).wait()


def _dispatch(dest, x2, zeros, td):
    t = x2.shape[0]
    return pl.pallas_call(
        functools.partial(_dispatch_kernel, td=td),
        grid=(t // td,),
        in_specs=[pl.BlockSpec((TOP_K, td), lambda i: (0, i), memory_space=pltpu.SMEM),
                  pl.BlockSpec((td, D_MODEL), lambda i: (i, 0)),
                  pl.BlockSpec(memory_space=pl.ANY)],
        out_specs=pl.BlockSpec(memory_space=pl.ANY),
        out_shape=jax.ShapeDtypeStruct(zeros.shape, zeros.dtype),
        input_output_aliases={2: 0},
        scratch_shapes=[pltpu.VMEM((td, X_SLAB, LANES), jnp.uint32), pltpu.SemaphoreType.DMA],
        compiler_params=_cparams(("arbitrary",)),
        name="moe_dispatch",
    )(dest, x2, zeros)


def _unpack_rows(words):
    return jnp.concatenate(_unpack_words(words), axis=1).astype(BF16)


def _swiglu(xb, w_gu, w_d):
    h = jnp.dot(xb, w_gu, preferred_element_type=F32)
    act = h[:, :EXPERT_FF] * jax.nn.sigmoid(h[:, :EXPERT_FF]) * h[:, EXPERT_FF:]
    return jnp.dot(act.astype(BF16), w_d, preferred_element_type=F32)


def _expert_kernel(blk_ref, exp_ref, used_ref, xs_ref, wgu_ref, wd_ref, ys_ref, wgu_b, wd_b, *, te):
    del blk_ref
    b = pl.program_id(0)
    live = b < used_ref[0]

    @pl.when(jnp.logical_or(b == 0, exp_ref[b] != exp_ref[jnp.maximum(b - 1, 0)]))
    def _():
        wgu_b[...] = wgu_ref[0, 0].astype(BF16)
        wd_b[...] = wd_ref[0, 0].astype(BF16)

    @pl.when(live)
    def _():
        words = jnp.concatenate(
            [xs_ref[pl.ds(j, te, stride=X_SLAB), :] for j in range(X_SLAB)], axis=1)
        y = _pack_rows(_swiglu(_unpack_rows(words), wgu_b[...], wd_b[...]))
        for c in range(X_SLAB):
            ys_ref[pl.ds(c, te, stride=X_SLAB), :] = y[:, c * LANES:(c + 1) * LANES]

    @pl.when(jnp.logical_not(live))
    def _():
        ys_ref[...] = jnp.zeros(ys_ref.shape, ys_ref.dtype)


def _experts(blk, exp, used, xs2, w_gu, w_d, layer, te):
    n_blocks = blk.shape[0]
    n_rows = xs2.shape[0] // X_SLAB
    return pl.pallas_call(
        functools.partial(_expert_kernel, te=te),
        grid_spec=pltpu.PrefetchScalarGridSpec(
            num_scalar_prefetch=3,
            grid=(n_blocks,),
            in_specs=[pl.BlockSpec((te * X_SLAB, LANES), lambda b, blk, exp, used: (blk[b], 0)),
                      pl.BlockSpec((1, 1, D_MODEL, 2 * EXPERT_FF),
                                   lambda b, blk, exp, used: (layer, exp[b], 0, 0)),
                      pl.BlockSpec((1, 1, EXPERT_FF, D_MODEL),
                                   lambda b, blk, exp, used: (layer, exp[b], 0, 0))],
            out_specs=pl.BlockSpec((te * X_SLAB, LANES), lambda b, blk, exp, used: (b, 0)),
            scratch_shapes=[pltpu.VMEM((D_MODEL, 2 * EXPERT_FF), BF16),
                            pltpu.VMEM((EXPERT_FF, D_MODEL), BF16)],
        ),
        out_shape=jax.ShapeDtypeStruct((n_rows * X_SLAB, LANES), jnp.uint32),
        compiler_params=_cparams(("arbitrary",)),
        name="moe_experts",
    )(blk, exp, used, xs2, w_gu, w_d)


def _combine_kernel(dest_ref, dest_next_ref, gate_ref, x_ref, wgu_ref, wd_ref, g_ref, b_ref, ys_ref, ys2_ref,
                    o_ref, ybuf_ref, sem, *, tc):
    i = pl.program_id(0)
    last = pl.num_programs(0) - 1
    slot = i % 2
    rows = TOP_K * tc

    def row_copy(d_ref, j, t, into):
        dst = ybuf_ref.at[pl.ds((into * rows + j * tc + t) * X_SLAB, X_SLAB)]
        return pltpu.make_async_copy(ys_ref.at[d_ref[j, t]], dst, sem.at[into])

    def wait_tile(into):
        pltpu.make_async_copy(ys2_ref.at[pl.ds(0, rows * X_SLAB)],
                              ybuf_ref.at[pl.ds(into * rows * X_SLAB, rows * X_SLAB)], sem.at[into]).wait()

    @pl.when(i == 0)
    def _():
        def issue(t, carry):
            for j in range(TOP_K):
                row_copy(dest_ref, j, t, 0).start(priority=j % 2)
            return carry
        lax.fori_loop(0, tc, issue, 0)

    wait_tile(slot)
    x = x_ref[...]
    gate = gate_ref[...]
    acc_lo = [None] * X_SLAB
    acc_hi = [None] * X_SLAB
    per_phase = tc // TOP_K
    for j in range(TOP_K):
        for t in range(j * per_phase, (j + 1) * per_phase):
            for jj in range(TOP_K):
                row_copy(dest_next_ref, jj, t, 1 - slot).start(priority=jj % 2)
        g_j = gate[:, j:j + 1]
        for c in range(X_SLAB):
            start = (slot * rows + j * tc) * X_SLAB + c
            lo, hi = _unpack_words(ybuf_ref[pl.ds(start, tc, stride=X_SLAB), :])
            acc_lo[c] = g_j * lo if j == 0 else acc_lo[c] + g_j * lo
            acc_hi[c] = g_j * hi if j == 0 else acc_hi[c] + g_j * hi
    routed = jnp.concatenate(acc_lo + acc_hi, axis=1)
    acc = ALPHA * x + routed + _swiglu(x.astype(BF16), wgu_ref[...], wd_ref[...])
    o_ref[...] = _layer_norm(acc, g_ref[...], b_ref[...])

    @pl.when(i == last)
    def _():
        wait_tile(1 - slot)


def _combine(dest, gate_tm, x2, ys, ws_gu, ws_d, g, b, tc):
    t = x2.shape[0]
    n = t // tc
    fixed = lambda i: (0, 0)
    row = lambda i: (i, 0)
    smem = lambda f: pl.BlockSpec((TOP_K, tc), f, memory_space=pltpu.SMEM)
    return pl.pallas_call(
        functools.partial(_combine_kernel, tc=tc),
        grid=(n,),
        in_specs=[smem(lambda i: (0, i)), smem(lambda i: (0, jnp.minimum(i + 1, n - 1))),
                  pl.BlockSpec((tc, LANES), row), pl.BlockSpec((tc, D_MODEL), row),
                  pl.BlockSpec((D_MODEL, 2 * EXPERT_FF), fixed), pl.BlockSpec((EXPERT_FF, D_MODEL), fixed),
                  pl.BlockSpec((1, D_MODEL), fixed), pl.BlockSpec((1, D_MODEL), fixed),
                  pl.BlockSpec(memory_space=pl.ANY), pl.BlockSpec(memory_space=pl.ANY)],
        out_specs=pl.BlockSpec((tc, D_MODEL), row),
        out_shape=jax.ShapeDtypeStruct((t, D_MODEL), F32),
        scratch_shapes=[pltpu.VMEM((2 * TOP_K * tc * X_SLAB, LANES), jnp.uint32),
                        pltpu.SemaphoreType.DMA((2,))],
        compiler_params=_cparams(("arbitrary",)),
        name="moe_combine",
    )(dest, dest, gate_tm, x2, ws_gu, ws_d, g, b, ys.reshape(-1, X_SLAB, LANES), ys)


def _moe_layer(x2, w_router, bias, w_gu, w_d, layer, ws_gu, ws_d, ln_g, ln_b, sorted_buf=None,
               te=512, tr=512, td=512, tc=128):
    t = x2.shape[0]
    n_blocks = t * TOP_K // te + N_EXPERTS
    n_rows = n_blocks * te
    if sorted_buf is None:
        sorted_buf = jnp.zeros((n_rows, X_SLAB, LANES), jnp.uint32)
    idx, rank, gate_tm, cnt = _router(x2, w_router.T.astype(F32), bias.reshape(-1, 1).astype(F32), tr)

    counts = cnt[:, 0].astype(jnp.int32)
    padded = (counts + te - 1) // te * te
    pad_end = jnp.cumsum(padded)
    pad_start = pad_end - padded
    dest = _slots(pad_start, idx, rank, min(t, 4096))
    used = pad_end[-1] // te
    blk = jnp.minimum(jnp.arange(n_blocks, dtype=jnp.int32), used - 1)
    exp = jnp.sum((pad_end[None, :] // te <= blk[:, None]).astype(jnp.int32), axis=1)
    exp = jnp.minimum(exp, N_EXPERTS - 1)

    xs = _dispatch(dest, x2, sorted_buf, td)
    ys = _experts(blk, exp.astype(jnp.int32), used.reshape(1), xs.reshape(n_rows * X_SLAB, LANES),
                  w_gu, w_d, layer, te)
    out = _combine(dest, gate_tm, x2, ys, ws_gu.astype(BF16), ws_d.astype(BF16),
                   ln_g.reshape(1, -1), ln_b.reshape(1, -1), tc)
    return out, xs


def kernel(x, attn_w_in, attn_w_out, attn_lambda, attn_subln, hgrn_w_in, hgrn_w_out, hgrn_lower_bounds, hgrn_norm, moe_router, moe_router_bias, moe_w_gate_up, moe_w_down, shared_w_gate_up, shared_w_down, ln_gain, ln_bias):
    bsz, seq, _ = x.shape
    tables = _rotary_tables(seq)
    lb = jax.nn.softmax(hgrn_lower_bounds.astype(F32), axis=0)
    lb = jnp.cumsum(lb, axis=0) - lb[0]
    x2 = x.reshape(bsz * seq, D_MODEL)
    sorted_buf = None
    for layer in range(DEPTH):
        j = layer // 2
        if layer % 2 == 0:
            lambda_init = 0.8 - 0.6 * math.exp(-0.3 * layer)
            x2 = _attn_layer(x2, bsz, seq, attn_w_in[j], attn_w_out[j], attn_lambda[j], attn_subln[j],
                             lambda_init, ln_gain[layer, 0], ln_bias[layer, 0], tables)
        else:
            x2 = _hgrn_layer(x2, bsz, seq, hgrn_w_in[j], hgrn_w_out[j], lb[layer], hgrn_norm[j],
                             ln_gain[layer, 0], ln_bias[layer, 0])
        x2, sorted_buf = _moe_layer(x2, moe_router[layer], moe_router_bias[layer], moe_w_gate_up,
                                    moe_w_down, layer, shared_w_gate_up[layer], shared_w_down[layer],
                                    ln_gain[layer, 1], ln_bias[layer, 1], sorted_buf)
    return x2.reshape(bsz, seq, D_MODEL)
```

```python
import functools
import math

import jax
import jax.numpy as jnp
from jax import lax
from jax.experimental import pallas as pl
from jax.experimental.pallas import tpu as pltpu

F32 = jnp.float32
BF16 = jnp.bfloat16

D_MODEL = 1024
DEPTH = 4
CHUNK = 64
A_HEADS = 8
A_HEAD_DIM = 64
ROT_DIM = 16
ROPE_THETA = 500000.0
H_HEADS = 8
H_EXPAND = 128
N_EXPERTS = 64
N_GROUPS = 8
TOPK_GROUP = 4
TOP_K = 8
EXPERT_FF = 256
ROUTED_SCALE = 2.5
ALPHA = (2 * DEPTH) ** 0.25
LN_EPS = 1e-5
LANES = 128
VMEM_LIMIT = 48 * 1024 * 1024


def _cparams(sem):
    return pltpu.CompilerParams(dimension_semantics=sem, vmem_limit_bytes=VMEM_LIMIT)


def _layer_norm(y, g, b):
    mu = jnp.mean(y, axis=-1, keepdims=True)
    yc = y - mu
    var = jnp.mean(yc * yc, axis=-1, keepdims=True)
    return yc * lax.rsqrt(var + LN_EPS) * g + b


def _dot_nt(a, b):
    return lax.dot_general(a, b, (((1,), (1,)), ((), ())), preferred_element_type=F32)


def _dot_tn(a, b):
    return lax.dot_general(a, b, (((0,), (0,)), ((), ())), preferred_element_type=F32)


def _attn_proj_kernel(x_ref, w_ref, cf_ref, s1_ref, s2_ref, q_ref, k_ref, v_ref):
    xb = x_ref[...].astype(BF16)
    cf, s1, s2 = cf_ref[...], s1_ref[...], s2_ref[...]

    def rotary(y):
        outs = []
        for c in range(D_MODEL // LANES):
            yc = y[:, c * LANES:(c + 1) * LANES]
            outs.append(yc * cf + pltpu.roll(yc, LANES - ROT_DIM // 2, 1) * s1
                        + pltpu.roll(yc, ROT_DIM // 2, 1) * s2)
        return jnp.concatenate(outs, axis=1)

    q = jnp.dot(xb, w_ref[:, 0:D_MODEL], preferred_element_type=F32)
    q_ref[...] = (rotary(q) * (A_HEAD_DIM ** -0.5 * math.log2(math.e))).astype(BF16)
    k = jnp.dot(xb, w_ref[:, D_MODEL:2 * D_MODEL], preferred_element_type=F32)
    k_ref[...] = rotary(k).astype(BF16)
    v = jnp.dot(xb, w_ref[:, 2 * D_MODEL:3 * D_MODEL], preferred_element_type=F32)
    v_ref[...] = v.astype(BF16)


def _attn_proj(x2, w, cf, s1, s2, seq, tm):
    t = x2.shape[0]
    nseq = seq // tm
    row = lambda i: (i, 0)
    tab = lambda i: (i % nseq, 0)
    out = jax.ShapeDtypeStruct((t, D_MODEL), BF16)
    return pl.pallas_call(
        _attn_proj_kernel,
        grid=(t // tm,),
        in_specs=[pl.BlockSpec((tm, D_MODEL), row),
                  pl.BlockSpec((D_MODEL, 3 * D_MODEL), lambda i: (0, 0)),
                  pl.BlockSpec((tm, LANES), tab), pl.BlockSpec((tm, LANES), tab),
                  pl.BlockSpec((tm, LANES), tab)],
        out_specs=[pl.BlockSpec((tm, D_MODEL), row)] * 3,
        out_shape=[out, out, out],
        compiler_params=_cparams(("arbitrary",)),
        name="attn_proj",
    )(x2, w, cf, s1, s2)


ATTN_WIDE = 6


def _attn_kernel(lam_ref, q_ref, k_ref, v_ref, g_ref, o_ref, m1_ref, a1_ref, m2_ref, a2_ref,
                 *, tq, out_scale):
    i = pl.program_id(2)
    q = q_ref[0]
    lane = lax.broadcasted_iota(jnp.int32, q.shape, 1)
    zero = jnp.zeros_like(q)
    q1 = jnp.where(lane < A_HEAD_DIM, q, zero)
    q2 = jnp.where(lane >= A_HEAD_DIM, q, zero)
    q12 = jnp.concatenate([q1, q2], axis=0)

    for m_ref, a_ref in ((m1_ref, a1_ref), (m2_ref, a2_ref)):
        m_ref[...] = jnp.full(m_ref.shape, -jnp.inf, F32)
        a_ref[...] = jnp.zeros(a_ref.shape, F32)

    def update(s, vb1, m_ref, a_ref):
        tiles = [s[:, c * LANES:(c + 1) * LANES] for c in range(s.shape[1] // LANES)]
        mc = functools.reduce(jnp.maximum, tiles)
        m_old = m_ref[...]
        m_new = jnp.maximum(m_old, jnp.max(mc, axis=-1, keepdims=True))
        alpha = jnp.exp2(m_old - m_new)
        pb = jnp.concatenate([jnp.exp2((t - m_new).astype(BF16)) for t in tiles], axis=1)
        pv = jnp.dot(pb, vb1, preferred_element_type=F32)
        a_ref[:, :LANES] = alpha * a_ref[:, :LANES] + pv[:, :LANES]
        a_ref[:, LANES:] = alpha * a_ref[:, LANES:] + pv[:, LANES:]
        m_ref[...] = m_new

    def block(j, width, diagonal):
        start = pl.multiple_of(j * tq, tq)
        kb = k_ref[0, pl.ds(start, width * tq), :]
        vb1 = jnp.concatenate([v_ref[0, pl.ds(start, width * tq), :],
                               jnp.ones((width * tq, LANES), BF16)], axis=1)
        s = _dot_nt(q12, kb)
        s1, s2 = s[:tq], s[tq:]
        if diagonal:
            rq = lax.broadcasted_iota(jnp.int32, (tq, width * tq), 0) // CHUNK
            ck = lax.broadcasted_iota(jnp.int32, (tq, width * tq), 1) // CHUNK
            mask = ck <= rq + (width - 1) * (tq // CHUNK)
            s1 = jnp.where(mask, s1, -jnp.inf)
            s2 = jnp.where(mask, s2, -jnp.inf)
        update(s1, vb1, m1_ref, a1_ref)
        update(s2, vb1, m2_ref, a2_ref)

    def wide_block(p, carry):
        block(ATTN_WIDE * p, ATTN_WIDE, False)
        return carry

    lax.fori_loop(0, i // ATTN_WIDE, wide_block, 0)
    for rem in range(ATTN_WIDE):
        @pl.when(i % ATTN_WIDE == rem)
        def _(rem=rem):
            block(i - rem, rem + 1, True)


    lam = lam_ref[0]
    o = a1_ref[:, :LANES] / a1_ref[:, LANES:] - lam * (a2_ref[:, :LANES] / a2_ref[:, LANES:])
    o = o * lax.rsqrt(jnp.mean(o * o, axis=-1, keepdims=True) + LN_EPS) * g_ref[...]
    o_ref[0] = (o * out_scale).astype(o_ref.dtype)


def _attention(lam, q, k, v, g, lambda_init, tq):
    b, s, _ = q.shape
    kern = functools.partial(_attn_kernel, tq=tq, out_scale=1.0 - lambda_init)
    kv_spec = pl.BlockSpec((1, s, 2 * A_HEAD_DIM), lambda bi, h, i: (bi, 0, h))
    qo_spec = pl.BlockSpec((1, tq, 2 * A_HEAD_DIM), lambda bi, h, i: (bi, i, h))
    vec = lambda n: pltpu.VMEM((tq, n), F32)
    return pl.pallas_call(
        kern,
        grid=(b, A_HEADS, s // tq),
        in_specs=[pl.BlockSpec(memory_space=pltpu.SMEM), qo_spec, kv_spec, kv_spec,
                  pl.BlockSpec((1, 2 * A_HEAD_DIM), lambda bi, h, i: (0, 0))],
        out_specs=qo_spec,
        out_shape=jax.ShapeDtypeStruct((b, s, D_MODEL), BF16),
        scratch_shapes=[vec(LANES), vec(2 * LANES)] * 2,
        compiler_params=_cparams(("arbitrary", "arbitrary", "arbitrary")),
        name="diff_attention",
    )(lam, q, k, v, g)


def _out_proj_kernel(o_ref, x_ref, w_ref, g_ref, b_ref, y_ref):
    h = jnp.dot(o_ref[...], w_ref[...], preferred_element_type=F32)
    y_ref[...] = _layer_norm(ALPHA * x_ref[...] + h, g_ref[...], b_ref[...])


def _out_proj(o2, x2, w, g, b, tm):
    t = x2.shape[0]
    row = lambda i: (i, 0)
    fixed = lambda i: (0, 0)
    return pl.pallas_call(
        _out_proj_kernel,
        grid=(t // tm,),
        in_specs=[pl.BlockSpec((tm, D_MODEL), row), pl.BlockSpec((tm, D_MODEL), row),
                  pl.BlockSpec((D_MODEL, D_MODEL), fixed),
                  pl.BlockSpec((1, D_MODEL), fixed), pl.BlockSpec((1, D_MODEL), fixed)],
        out_specs=pl.BlockSpec((tm, D_MODEL), row),
        out_shape=jax.ShapeDtypeStruct((t, D_MODEL), F32),
        compiler_params=_cparams(("arbitrary",)),
        name="out_proj_ln",
    )(o2, x2, w, g, b)


def _rotary_tables(seq):
    pos = jnp.arange(seq, dtype=F32)
    inv_freq = ROPE_THETA ** (-jnp.arange(0, ROT_DIM, 2, dtype=F32) / ROT_DIM)
    ang = pos[:, None] * inv_freq[None, :]
    cos, sin = jnp.cos(ang), jnp.sin(ang)
    half = ROT_DIM // 2
    pad = A_HEAD_DIM - ROT_DIM
    one = jnp.ones((seq, pad), F32)
    zero = jnp.zeros((seq, pad), F32)
    zh = jnp.zeros((seq, half), F32)
    cf = jnp.concatenate([cos, cos, one], axis=1)
    s1 = jnp.concatenate([-sin, zh, zero], axis=1)
    s2 = jnp.concatenate([zh, sin, zero], axis=1)
    rep = LANES // A_HEAD_DIM
    return tuple(jnp.tile(t, (1, rep)) for t in (cf, s1, s2))


def _attn_layer(x2, bsz, seq, w_in, w_out, lam_params, subln, lambda_init, ln_g, ln_b, tables,
                tm=512, tq=512):
    q, k, v = _attn_proj(x2, w_in.astype(BF16), *tables, seq, tm)
    lp = lam_params.astype(F32)
    lam = jnp.exp(jnp.sum(lp[0] * lp[1])) - jnp.exp(jnp.sum(lp[2] * lp[3])) + lambda_init
    shp = (bsz, seq, D_MODEL)
    o = _attention(lam.reshape(1), q.reshape(shp), k.reshape(shp), v.reshape(shp),
                   subln.reshape(1, -1).astype(F32), lambda_init, tq)
    return _out_proj(o.reshape(bsz * seq, D_MODEL), x2, w_out.astype(BF16),
                     ln_g.reshape(1, -1), ln_b.reshape(1, -1), tm)


H_CHUNK = 128
H_SUB = 16


def _hgrn_proj_kernel(x_ref, w_ref, lb_ref, q_ref, k_ref, v_ref, lf_ref, g_ref):
    xb = x_ref[...].astype(BF16)
    w = H_HEADS * H_EXPAND
    q = jnp.dot(xb, w_ref[:, 0:w], preferred_element_type=F32)
    q_ref[...] = q * jax.nn.sigmoid(q)
    f = jnp.dot(xb, w_ref[:, w:2 * w], preferred_element_type=F32)
    lb = lb_ref[...]
    forget = lb + (1.0 - lb) * jax.nn.sigmoid(f)
    lf_ref[...] = jnp.log(forget)
    k_ref[...] = 1.0 - forget
    v_ref[...] = jnp.dot(xb, w_ref[:, 2 * w:3 * w], preferred_element_type=F32)
    g = jnp.dot(xb, w_ref[:, 3 * w:4 * w], preferred_element_type=F32)
    g_ref[...] = g * jax.nn.sigmoid(g)


def _hgrn_proj(x2, w, lb, tm):
    t = x2.shape[0]
    row = lambda i: (i, 0)
    fixed = lambda i: (0, 0)
    out = jax.ShapeDtypeStruct((t, D_MODEL), F32)
    return pl.pallas_call(
        _hgrn_proj_kernel,
        grid=(t // tm,),
        in_specs=[pl.BlockSpec((tm, D_MODEL), row), pl.BlockSpec((D_MODEL, 4 * D_MODEL), fixed),
                  pl.BlockSpec((1, D_MODEL), fixed)],
        out_specs=[pl.BlockSpec((tm, D_MODEL), row)] * 5,
        out_shape=[out] * 5,
        compiler_params=_cparams(("arbitrary",)),
        name="hgrn_proj",
    )(x2, w, lb)


H_FAST_SPREAD = 96.0


def _hgrn_cumsum(lf):
    c = H_CHUNK
    r_i = lax.broadcasted_iota(jnp.int32, (c, c), 0)
    c_i = lax.broadcasted_iota(jnp.int32, (c, c), 1)
    tri = (r_i >= c_i).astype(BF16)
    hi = lf.astype(BF16)
    lo = (lf - hi.astype(F32)).astype(BF16)
    b2 = jnp.dot(tri, jnp.concatenate([hi, lo], axis=1), preferred_element_type=F32)
    return (b2[:, :H_EXPAND] + b2[:, H_EXPAND:]) * math.log2(math.e)


def _hgrn_spread(b):
    nb = H_CHUNK // H_SUB
    ends = [b[(j + 1) * H_SUB - 1:(j + 1) * H_SUB, :] for j in range(nb)]
    drops = [-ends[0]] + [ends[j - 1] - ends[j] for j in range(1, nb)]
    return functools.reduce(jnp.maximum, drops)


def _hgrn_chunk(q, k, v, b, st, factored):
    c, nb = H_CHUNK, H_CHUNK // H_SUB
    r_i = lax.broadcasted_iota(jnp.int32, (c, c), 0)
    c_i = lax.broadcasted_iota(jnp.int32, (c, c), 1)
    b_last = b[c - 1:c, :]

    o = _dot_nt((q * jnp.exp2(b)).astype(BF16), st.astype(BF16))
    ke = k * jnp.exp2(b_last - b)
    st_new = st * jnp.exp2(b_last) + _dot_tn(v.astype(BF16), ke.astype(BF16))

    ends = [b[(j + 1) * H_SUB - 1:(j + 1) * H_SUB, :] for j in range(nb)]
    e_blk = jnp.concatenate([jnp.broadcast_to(e, (H_SUB, H_EXPAND)) for e in ends], axis=0)
    kt = k * jnp.exp2(e_blk - b)
    row_blk = lax.broadcasted_iota(jnp.int32, (c, H_EXPAND), 0) // H_SUB
    q_big = jnp.concatenate(
        [(q * jnp.exp2(jnp.minimum(b - ends[j], 0.0))).astype(BF16) for j in range(nb)], axis=1)
    k_big = jnp.concatenate(
        [jnp.where(row_blk == j, kt, 0.0).astype(BF16) for j in range(nb)], axis=1)
    a = _dot_nt(q_big, k_big)
    a = jnp.where(r_i // H_SUB > c_i // H_SUB, a, 0.0)

    if factored:
        a_d = _dot_nt((q * jnp.exp2(b - e_blk)).astype(BF16), kt.astype(BF16))
        a = a + jnp.where((r_i // H_SUB == c_i // H_SUB) & (r_i >= c_i), a_d, 0.0)
        return o + jnp.dot(a.astype(BF16), v.astype(BF16), preferred_element_type=F32), st_new

    o = o + jnp.dot(a.astype(BF16), v.astype(BF16), preferred_element_type=F32)
    t_idx = lax.broadcasted_iota(jnp.int32, (H_SUB, 1), 0)
    diag = []
    for j in range(nb):
        sl = slice(j * H_SUB, (j + 1) * H_SUB)
        qb, kb, vb, bb = q[sl], k[sl], v[sl], b[sl]
        od = jnp.zeros((H_SUB, H_EXPAND), F32)
        for s in range(H_SUB):
            e = jnp.exp2(bb - bb[s:s + 1, :])
            col = jnp.sum(qb * (kb[s:s + 1, :] * e), axis=-1, keepdims=True)
            od = od + jnp.where(t_idx >= s, col, 0.0) * vb[s:s + 1, :]
        diag.append(od)
    return o + jnp.concatenate(diag, axis=0), st_new


def _hgrn_kernel(q_ref, k_ref, v_ref, lf_ref, g_ref, ng_ref, o_ref, st_ref, b_ref, *, n_chunks):
    @pl.when(pl.program_id(2) == 0)
    def _():
        st_ref[...] = jnp.zeros(st_ref.shape, F32)

    spread = None
    for ci in range(n_chunks):
        sl = slice(ci * H_CHUNK, (ci + 1) * H_CHUNK)
        b = _hgrn_cumsum(lf_ref[sl, :])
        b_ref[sl, :] = b
        sp = _hgrn_spread(b)
        spread = sp if spread is None else jnp.maximum(spread, sp)
    mild = jnp.max(spread) < H_FAST_SPREAD

    def run(factored):
        st = st_ref[...]
        for ci in range(n_chunks):
            sl = slice(ci * H_CHUNK, (ci + 1) * H_CHUNK)
            o, st = _hgrn_chunk(q_ref[sl, :], k_ref[sl, :], v_ref[sl, :], b_ref[sl, :], st, factored)
            o = o * lax.rsqrt(jnp.mean(o * o, axis=-1, keepdims=True) + LN_EPS) * ng_ref[...]
            o_ref[sl, :] = (o * g_ref[sl, :]).astype(o_ref.dtype)
        st_ref[...] = st

    @pl.when(mild)
    def _():
        run(True)

    @pl.when(jnp.logical_not(mild))
    def _():
        run(False)


def _hgrn_recurrence(q, k, v, lf, g, ng, bsz, seq, tc):
    t = q.shape[0]
    nt = seq // tc
    blk = pl.BlockSpec((tc, H_EXPAND), lambda b, h, c: (b * nt + c, h))
    return pl.pallas_call(
        functools.partial(_hgrn_kernel, n_chunks=tc // H_CHUNK),
        grid=(bsz, H_HEADS, nt),
        in_specs=[blk] * 5 + [pl.BlockSpec((1, H_EXPAND), lambda b, h, c: (0, 0))],
        out_specs=blk,
        out_shape=jax.ShapeDtypeStruct((t, D_MODEL), BF16),
        scratch_shapes=[pltpu.VMEM((H_EXPAND, H_EXPAND), F32), pltpu.VMEM((tc, H_EXPAND), F32)],
        compiler_params=_cparams(("arbitrary", "arbitrary", "arbitrary")),
        name="hgrn_recurrence",
    )(q, k, v, lf, g, ng)


def _hgrn_layer(x2, bsz, seq, w_in, w_out, lb, norm_g, ln_g, ln_b, tm=256, tc=1024):
    q, k, v, lf, g = _hgrn_proj(x2, w_in.astype(BF16), lb.reshape(1, -1).astype(F32), tm)
    o = _hgrn_recurrence(q, k, v, lf, g, norm_g.reshape(1, -1).astype(F32), bsz, seq, tc)
    return _out_proj(o, x2, w_out.astype(BF16), ln_g.reshape(1, -1), ln_b.reshape(1, -1), 2 * tm)


GROUP_SIZE = N_EXPERTS // N_GROUPS
X_WORDS = D_MODEL // 2
X_SLAB = X_WORDS // LANES


def _router_kernel(x_ref, wt_ref, bias_ref, idx_ref, rank_ref, gate_tm_ref, cnt_ref,
                   carry_ref, *, tr):
    @pl.when(pl.program_id(0) == 0)
    def _():
        carry_ref[...] = jnp.zeros(carry_ref.shape, F32)

    e_n = N_EXPERTS
    logits = lax.dot_general(wt_ref[...], x_ref[...], (((1,), (1,)), ((), ())),
                             precision=lax.Precision.HIGHEST, preferred_element_type=F32)
    scores = jax.nn.sigmoid(logits)
    choice = scores + bias_ref[...]

    ch3 = choice.reshape(N_GROUPS, GROUP_SIZE, tr)
    sub = lax.broadcasted_iota(jnp.int32, ch3.shape, 1)
    m1 = jnp.max(ch3, axis=1, keepdims=True)
    first = jnp.min(jnp.where(ch3 == m1, sub, GROUP_SIZE), axis=1, keepdims=True)
    m2 = jnp.max(jnp.where(sub == first, -jnp.inf, ch3), axis=1, keepdims=True)
    gs = (m1 + m2).reshape(N_GROUPS, tr)

    g_i = lax.broadcasted_iota(jnp.int32, gs.shape, 0)
    g_rank = jnp.zeros(gs.shape, F32)
    for g in range(N_GROUPS):
        row = gs[g:g + 1, :]
        ahead = (row > gs) | ((row == gs) & (g_i > g))
        g_rank = g_rank + jnp.where(ahead, 1.0, 0.0)
    g_keep = (g_rank < TOPK_GROUP).astype(F32).reshape(N_GROUPS, 1, tr)
    keep = jnp.broadcast_to(g_keep, (N_GROUPS, GROUP_SIZE, tr)).reshape(e_n, tr) > 0.5
    cm = jnp.where(keep, choice, -jnp.inf)

    e_i = lax.broadcasted_iota(jnp.int32, cm.shape, 0)
    sub_i = lax.broadcasted_iota(jnp.int32, (GROUP_SIZE, tr), 0)
    cm_g = [cm[g * GROUP_SIZE:(g + 1) * GROUP_SIZE, :] for g in range(N_GROUPS)]
    ranks = [jnp.zeros((GROUP_SIZE, tr), F32) for _ in range(N_GROUPS)]
    for e in range(e_n):
        ge, se = divmod(e, GROUP_SIZE)
        row = jnp.broadcast_to(cm_g[ge][se:se + 1, :], (GROUP_SIZE, tr))
        for g in range(N_GROUPS):
            if g > ge:
                ahead = row >= cm_g[g]
            elif g < ge:
                ahead = row > cm_g[g]
            else:
                ahead = (row > cm_g[g]) | ((row == cm_g[g]) & (sub_i > se))
            ranks[g] = ranks[g] + jnp.where(ahead, 1.0, 0.0)
    sel = jnp.concatenate(ranks, axis=0) < TOP_K
    sel_f = sel.astype(F32)
    w = jnp.where(sel, scores, 0.0)
    gate = w / (jnp.sum(w, axis=0, keepdims=True) + 1e-20) * ROUTED_SCALE

    sel_b = sel_f.astype(BF16)
    t_r = lax.broadcasted_iota(jnp.int32, (tr, tr), 0)
    t_c = lax.broadcasted_iota(jnp.int32, (tr, tr), 1)
    before = jnp.dot(sel_b, (t_r < t_c).astype(BF16), preferred_element_type=F32)
    tok_rank = carry_ref[:, 0:1] + before
    carry_ref[...] = carry_ref[...] + jnp.sum(sel_f, axis=1, keepdims=True)
    cnt_ref[...] = carry_ref[...]

    x_r = lax.broadcasted_iota(jnp.int32, (e_n, e_n), 0)
    x_c = lax.broadcasted_iota(jnp.int32, (e_n, e_n), 1)
    slot = jnp.dot((x_c < x_r).astype(BF16), sel_b, preferred_element_type=F32)
    e_f = e_i.astype(F32)
    idx_rows, gate_rows, rank_rows = [], [], []
    for j in range(TOP_K):
        pick = jnp.where(sel & (slot == j), 1.0, 0.0)
        idx_rows.append(jnp.sum(pick * e_f, axis=0, keepdims=True))
        gate_rows.append(jnp.sum(pick * gate, axis=0, keepdims=True))
        rank_rows.append(jnp.sum(pick * tok_rank, axis=0, keepdims=True))
    idx_ref[...] = jnp.concatenate(idx_rows, axis=0).astype(jnp.int32)
    rank_ref[...] = jnp.concatenate(rank_rows, axis=0).astype(jnp.int32)
    gates = jnp.concatenate(gate_rows, axis=0)
    padded = jnp.concatenate([gates, jnp.zeros((LANES - TOP_K, tr), F32)], axis=0)
    gate_tm_ref[...] = padded.T


def _router(x2, wt, bias, tr):
    t = x2.shape[0]
    col = lambda i: (0, i)
    return pl.pallas_call(
        functools.partial(_router_kernel, tr=tr),
        grid=(t // tr,),
        in_specs=[pl.BlockSpec((tr, D_MODEL), lambda i: (i, 0)),
                  pl.BlockSpec((N_EXPERTS, D_MODEL), lambda i: (0, 0)),
                  pl.BlockSpec((N_EXPERTS, 1), lambda i: (0, 0))],
        out_specs=[pl.BlockSpec((TOP_K, tr), col),
                   pl.BlockSpec((TOP_K, tr), col), pl.BlockSpec((tr, LANES), lambda i: (i, 0)),
                   pl.BlockSpec((N_EXPERTS, LANES), lambda i: (0, 0))],
        out_shape=[jax.ShapeDtypeStruct((TOP_K, t), jnp.int32),
                   jax.ShapeDtypeStruct((TOP_K, t), jnp.int32),
                   jax.ShapeDtypeStruct((t, LANES), F32),
                   jax.ShapeDtypeStruct((N_EXPERTS, LANES), F32)],
        scratch_shapes=[pltpu.VMEM((N_EXPERTS, LANES), F32)],
        compiler_params=_cparams(("arbitrary",)),
        name="moe_router",
    )(x2, wt, bias)


ISSUE_UNROLL = 8


def _pack_rows(x):
    lo = pltpu.bitcast(x[:, :X_WORDS].astype(BF16).astype(F32), jnp.uint32)
    hi = pltpu.bitcast(x[:, X_WORDS:].astype(BF16).astype(F32), jnp.uint32)
    return (lo >> 16) | (hi & jnp.uint32(0xFFFF0000))


def _unpack_words(words):
    return (pltpu.bitcast(words << 16, F32), pltpu.bitcast(words & jnp.uint32(0xFFFF0000), F32))


def _slots_kernel(start_ref, idx_ref, rank_ref, dest_ref):
    idx = idx_ref[...]
    dest = rank_ref[...]
    for e in range(N_EXPERTS):
        dest = dest + jnp.where(idx == e, start_ref[e], 0)
    dest_ref[...] = dest


def _slots(pad_start, idx, rank, tl):
    t = idx.shape[1]
    col = pl.BlockSpec((TOP_K, tl), lambda i: (0, i))
    return pl.pallas_call(
        _slots_kernel,
        grid=(t // tl,),
        in_specs=[pl.BlockSpec(memory_space=pltpu.SMEM), col, col],
        out_specs=col,
        out_shape=jax.ShapeDtypeStruct(idx.shape, jnp.int32),
        compiler_params=_cparams(("arbitrary",)),
        name="moe_slots",
    )(pad_start, idx, rank)


def _dispatch_kernel(dest_ref, x_ref, zeros_ref, xs_ref, xp_ref, sem, *, td):
    del zeros_ref
    words = _pack_rows(x_ref[...])
    for j in range(X_SLAB):
        xp_ref[:, j, :] = words[:, j * LANES:(j + 1) * LANES]

    def issue(g, carry):
        for u in range(ISSUE_UNROLL):
            t = g * ISSUE_UNROLL + u
            for j in range(TOP_K):
                pltpu.make_async_copy(xp_ref.at[t], xs_ref.at[dest_ref[j, t]], sem).start(priority=j % 2)
        return carry

    lax.fori_loop(0, td // ISSUE_UNROLL, issue, 0)
    for j in range(TOP_K):
        pltpu.make_async_copy(xp_ref, xs_ref.at[pl.ds(0, td)], sem).wait()


def _dispatch(dest, x2, zeros, td):
    t = x2.shape[0]
    return pl.pallas_call(
        functools.partial(_dispatch_kernel, td=td),
        grid=(t // td,),
        in_specs=[pl.BlockSpec((TOP_K, td), lambda i: (0, i), memory_space=pltpu.SMEM),
                  pl.BlockSpec((td, D_MODEL), lambda i: (i, 0)),
                  pl.BlockSpec(memory_space=pl.ANY)],
        out_specs=pl.BlockSpec(memory_space=pl.ANY),
        out_shape=jax.ShapeDtypeStruct(zeros.shape, zeros.dtype),
        input_output_aliases={2: 0},
        scratch_shapes=[pltpu.VMEM((td, X_SLAB, LANES), jnp.uint32), pltpu.SemaphoreType.DMA],
        compiler_params=_cparams(("arbitrary",)),
        name="moe_dispatch",
    )(dest, x2, zeros)


def _unpack_rows(words):
    return jnp.concatenate(_unpack_words(words), axis=1).astype(BF16)


def _swiglu(xb, w_gu, w_d):
    h = jnp.dot(xb, w_gu, preferred_element_type=F32)
    act = h[:, :EXPERT_FF] * jax.nn.sigmoid(h[:, :EXPERT_FF]) * h[:, EXPERT_FF:]
    return jnp.dot(act.astype(BF16), w_d, preferred_element_type=F32)


def _expert_kernel(blk_ref, exp_ref, used_ref, xs_ref, wgu_ref, wd_ref, ys_ref, wgu_b, wd_b, *, te):
    del blk_ref
    b = pl.program_id(0)
    live = b < used_ref[0]

    @pl.when(jnp.logical_or(b == 0, exp_ref[b] != exp_ref[jnp.maximum(b - 1, 0)]))
    def _():
        wgu_b[...] = wgu_ref[0, 0].astype(BF16)
        wd_b[...] = wd_ref[0, 0].astype(BF16)

    @pl.when(live)
    def _():
        words = jnp.concatenate(
            [xs_ref[pl.ds(j, te, stride=X_SLAB), :] for j in range(X_SLAB)], axis=1)
        y = _pack_rows(_swiglu(_unpack_rows(words), wgu_b[...], wd_b[...]))
        for c in range(X_SLAB):
            ys_ref[pl.ds(c, te, stride=X_SLAB), :] = y[:, c * LANES:(c + 1) * LANES]

    @pl.when(jnp.logical_not(live))
    def _():
        ys_ref[...] = jnp.zeros(ys_ref.shape, ys_ref.dtype)


def _experts(blk, exp, used, xs2, w_gu, w_d, layer, te):
    n_blocks = blk.shape[0]
    n_rows = xs2.shape[0] // X_SLAB
    return pl.pallas_call(
        functools.partial(_expert_kernel, te=te),
        grid_spec=pltpu.PrefetchScalarGridSpec(
            num_scalar_prefetch=3,
            grid=(n_blocks,),
            in_specs=[pl.BlockSpec((te * X_SLAB, LANES), lambda b, blk, exp, used: (blk[b], 0)),
                      pl.BlockSpec((1, 1, D_MODEL, 2 * EXPERT_FF),
                                   lambda b, blk, exp, used: (layer, exp[b], 0, 0)),
                      pl.BlockSpec((1, 1, EXPERT_FF, D_MODEL),
                                   lambda b, blk, exp, used: (layer, exp[b], 0, 0))],
            out_specs=pl.BlockSpec((te * X_SLAB, LANES), lambda b, blk, exp, used: (b, 0)),
            scratch_shapes=[pltpu.VMEM((D_MODEL, 2 * EXPERT_FF), BF16),
                            pltpu.VMEM((EXPERT_FF, D_MODEL), BF16)],
        ),
        out_shape=jax.ShapeDtypeStruct((n_rows * X_SLAB, LANES), jnp.uint32),
        compiler_params=_cparams(("arbitrary",)),
        name="moe_experts",
    )(blk, exp, used, xs2, w_gu, w_d)


def _combine_kernel(dest_ref, dest_next_ref, gate_ref, x_ref, wgu_ref, wd_ref, g_ref, b_ref, ys_ref, ys2_ref,
                    o_ref, ybuf_ref, sem, *, tc):
    i = pl.program_id(0)
    last = pl.num_programs(0) - 1
    slot = i % 2
    rows = TOP_K * tc

    def row_copy(d_ref, j, t, into):
        dst = ybuf_ref.at[pl.ds((into * rows + j * tc + t) * X_SLAB, X_SLAB)]
        return pltpu.make_async_copy(ys_ref.at[d_ref[j, t]], dst, sem.at[into])

    def wait_tile(into):
        pltpu.make_async_copy(ys2_ref.at[pl.ds(0, rows * X_SLAB)],
                              ybuf_ref.at[pl.ds(into * rows * X_SLAB, rows * X_SLAB)], sem.at[into]).wait()

    @pl.when(i == 0)
    def _():
        def issue(t, carry):
            for j in range(TOP_K):
                row_copy(dest_ref, j, t, 0).start(priority=j % 2)
            return carry
        lax.fori_loop(0, tc, issue, 0)

    wait_tile(slot)
    x = x_ref[...]
    gate = gate_ref[...]
    acc_lo = [None] * X_SLAB
    acc_hi = [None] * X_SLAB
    per_phase = tc // TOP_K
    for j in range(TOP_K):
        for t in range(j * per_phase, (j + 1) * per_phase):
            for jj in range(TOP_K):
                row_copy(dest_next_ref, jj, t, 1 - slot).start(priority=jj % 2)
        g_j = gate[:, j:j + 1]
        for c in range(X_SLAB):
            start = (slot * rows + j * tc) * X_SLAB + c
            lo, hi = _unpack_words(ybuf_ref[pl.ds(start, tc, stride=X_SLAB), :])
            acc_lo[c] = g_j * lo if j == 0 else acc_lo[c] + g_j * lo
            acc_hi[c] = g_j * hi if j == 0 else acc_hi[c] + g_j * hi
    routed = jnp.concatenate(acc_lo + acc_hi, axis=1)
    acc = ALPHA * x + routed + _swiglu(x.astype(BF16), wgu_ref[...], wd_ref[...])
    o_ref[...] = _layer_norm(acc, g_ref[...], b_ref[...])

    @pl.when(i == last)
    def _():
        wait_tile(1 - slot)


def _combine(dest, gate_tm, x2, ys, ws_gu, ws_d, g, b, tc):
    t = x2.shape[0]
    n = t // tc
    fixed = lambda i: (0, 0)
    row = lambda i: (i, 0)
    smem = lambda f: pl.BlockSpec((TOP_K, tc), f, memory_space=pltpu.SMEM)
    return pl.pallas_call(
        functools.partial(_combine_kernel, tc=tc),
        grid=(n,),
        in_specs=[smem(lambda i: (0, i)), smem(lambda i: (0, jnp.minimum(i + 1, n - 1))),
                  pl.BlockSpec((tc, LANES), row), pl.BlockSpec((tc, D_MODEL), row),
                  pl.BlockSpec((D_MODEL, 2 * EXPERT_FF), fixed), pl.BlockSpec((EXPERT_FF, D_MODEL), fixed),
                  pl.BlockSpec((1, D_MODEL), fixed), pl.BlockSpec((1, D_MODEL), fixed),
                  pl.BlockSpec(memory_space=pl.ANY), pl.BlockSpec(memory_space=pl.ANY)],
        out_specs=pl.BlockSpec((tc, D_MODEL), row),
        out_shape=jax.ShapeDtypeStruct((t, D_MODEL), F32),
        scratch_shapes=[pltpu.VMEM((2 * TOP_K * tc * X_SLAB, LANES), jnp.uint32),
                        pltpu.SemaphoreType.DMA((2,))],
        compiler_params=_cparams(("arbitrary",)),
        name="moe_combine",
    )(dest, dest, gate_tm, x2, ws_gu, ws_d, g, b, ys.reshape(-1, X_SLAB, LANES), ys)


def _moe_layer(x2, w_router, bias, w_gu, w_d, layer, ws_gu, ws_d, ln_g, ln_b, sorted_buf=None,
               te=512, tr=512, td=512, tc=128):
    t = x2.shape[0]
    n_blocks = t * TOP_K // te + N_EXPERTS
    n_rows = n_blocks * te
    if sorted_buf is None:
        sorted_buf = jnp.zeros((n_rows, X_SLAB, LANES), jnp.uint32)
    idx, rank, gate_tm, cnt = _router(x2, w_router.T.astype(F32), bias.reshape(-1, 1).astype(F32), tr)

    counts = cnt[:, 0].astype(jnp.int32)
    padded = (counts + te - 1) // te * te
    pad_end = jnp.cumsum(padded)
    pad_start = pad_end - padded
    dest = _slots(pad_start, idx, rank, min(t, 4096))
    used = pad_end[-1] // te
    blk = jnp.minimum(jnp.arange(n_blocks, dtype=jnp.int32), used - 1)
    exp = jnp.sum((pad_end[None, :] // te <= blk[:, None]).astype(jnp.int32), axis=1)
    exp = jnp.minimum(exp, N_EXPERTS - 1)

    xs = _dispatch(dest, x2, sorted_buf, td)
    ys = _experts(blk, exp.astype(jnp.int32), used.reshape(1), xs.reshape(n_rows * X_SLAB, LANES),
                  w_gu, w_d, layer, te)
    out = _combine(dest, gate_tm, x2, ys, ws_gu.astype(BF16), ws_d.astype(BF16),
                   ln_g.reshape(1, -1), ln_b.reshape(1, -1), tc)
    return out, xs


def kernel(x, attn_w_in, attn_w_out, attn_lambda, attn_subln, hgrn_w_in, hgrn_w_out, hgrn_lower_bounds, hgrn_norm, moe_router, moe_router_bias, moe_w_gate_up, moe_w_down, shared_w_gate_up, shared_w_down, ln_gain, ln_bias):
    bsz, seq, _ = x.shape
    tables = _rotary_tables(seq)
    lb = jax.nn.softmax(hgrn_lower_bounds.astype(F32), axis=0)
    lb = jnp.cumsum(lb, axis=0) - lb[0]
    x2 = x.reshape(bsz * seq, D_MODEL)
    sorted_buf = None
    for layer in range(DEPTH):
        j = layer // 2
        if layer % 2 == 0:
            lambda_init = 0.8 - 0.6 * math.exp(-0.3 * layer)
            x2 = _attn_layer(x2, bsz, seq, attn_w_in[j], attn_w_out[j], attn_lambda[j], attn_subln[j],
                             lambda_init, ln_gain[layer, 0], ln_bias[layer, 0], tables)
        else:
            x2 = _hgrn_layer(x2, bsz, seq, hgrn_w_in[j], hgrn_w_out[j], lb[layer], hgrn_norm[j],
                             ln_gain[layer, 0], ln_bias[layer, 0])
        x2, sorted_buf = _moe_layer(x2, moe_router[layer], moe_router_bias[layer], moe_w_gate_up,
                                    moe_w_down, layer, shared_w_gate_up[layer], shared_w_down[layer],
                                    ln_gain[layer, 1], ln_bias[layer, 1], sorted_buf)
    return x2.reshape(bsz, seq, D_MODEL)
```

```python
import functools
import math

import jax
import jax.numpy as jnp
from jax import lax
from jax.experimental import pallas as pl
from jax.experimental.pallas import tpu as pltpu

F32 = jnp.float32
BF16 = jnp.bfloat16

D_MODEL = 1024
DEPTH = 4
CHUNK = 64
A_HEADS = 8
A_HEAD_DIM = 64
ROT_DIM = 16
ROPE_THETA = 500000.0
H_HEADS = 8
H_EXPAND = 128
N_EXPERTS = 64
N_GROUPS = 8
TOPK_GROUP = 4
TOP_K = 8
EXPERT_FF = 256
ROUTED_SCALE = 2.5
ALPHA = (2 * DEPTH) ** 0.25
LN_EPS = 1e-5
LANES = 128
VMEM_LIMIT = 48 * 1024 * 1024


def _cparams(sem):
    return pltpu.CompilerParams(dimension_semantics=sem, vmem_limit_bytes=VMEM_LIMIT)


def _layer_norm(y, g, b):
    mu = jnp.mean(y, axis=-1, keepdims=True)
    yc = y - mu
    var = jnp.mean(yc * yc, axis=-1, keepdims=True)
    return yc * lax.rsqrt(var + LN_EPS) * g + b


def _dot_nt(a, b):
    return lax.dot_general(a, b, (((1,), (1,)), ((), ())), preferred_element_type=F32)


def _dot_tn(a, b):
    return lax.dot_general(a, b, (((0,), (0,)), ((), ())), preferred_element_type=F32)


def _attn_proj_kernel(x_ref, w_ref, cf_ref, s1_ref, s2_ref, q_ref, k_ref, v_ref):
    xb = x_ref[...].astype(BF16)
    cf, s1, s2 = cf_ref[...], s1_ref[...], s2_ref[...]

    def rotary(y):
        outs = []
        for c in range(D_MODEL // LANES):
            yc = y[:, c * LANES:(c + 1) * LANES]
            outs.append(yc * cf + pltpu.roll(yc, LANES - ROT_DIM // 2, 1) * s1
                        + pltpu.roll(yc, ROT_DIM // 2, 1) * s2)
        return jnp.concatenate(outs, axis=1)

    q = jnp.dot(xb, w_ref[:, 0:D_MODEL], preferred_element_type=F32)
    q_ref[...] = (rotary(q) * (A_HEAD_DIM ** -0.5 * math.log2(math.e))).astype(BF16)
    k = jnp.dot(xb, w_ref[:, D_MODEL:2 * D_MODEL], preferred_element_type=F32)
    k_ref[...] = rotary(k).astype(BF16)
    v = jnp.dot(xb, w_ref[:, 2 * D_MODEL:3 * D_MODEL], preferred_element_type=F32)
    v_ref[...] = v.astype(BF16)


def _attn_proj(x2, w, cf, s1, s2, seq, tm):
    t = x2.shape[0]
    nseq = seq // tm
    row = lambda i: (i, 0)
    tab = lambda i: (i % nseq, 0)
    out = jax.ShapeDtypeStruct((t, D_MODEL), BF16)
    return pl.pallas_call(
        _attn_proj_kernel,
        grid=(t // tm,),
        in_specs=[pl.BlockSpec((tm, D_MODEL), row),
                  pl.BlockSpec((D_MODEL, 3 * D_MODEL), lambda i: (0, 0)),
                  pl.BlockSpec((tm, LANES), tab), pl.BlockSpec((tm, LANES), tab),
                  pl.BlockSpec((tm, LANES), tab)],
        out_specs=[pl.BlockSpec((tm, D_MODEL), row)] * 3,
        out_shape=[out, out, out],
        compiler_params=_cparams(("arbitrary",)),
        name="attn_proj",
    )(x2, w, cf, s1, s2)


ATTN_WIDE = 6


def _attn_kernel(lam_ref, q_ref, k_ref, v_ref, g_ref, o_ref, m1_ref, a1_ref, m2_ref, a2_ref,
                 *, tq, out_scale):
    i = pl.program_id(2)
    q = q_ref[0]
    lane = lax.broadcasted_iota(jnp.int32, q.shape, 1)
    zero = jnp.zeros_like(q)
    q1 = jnp.where(lane < A_HEAD_DIM, q, zero)
    q2 = jnp.where(lane >= A_HEAD_DIM, q, zero)
    q12 = jnp.concatenate([q1, q2], axis=0)

    for m_ref, a_ref in ((m1_ref, a1_ref), (m2_ref, a2_ref)):
        m_ref[...] = jnp.full(m_ref.shape, -jnp.inf, F32)
        a_ref[...] = jnp.zeros(a_ref.shape, F32)

    def update(s, vb1, m_ref, a_ref):
        tiles = [s[:, c * LANES:(c + 1) * LANES] for c in range(s.shape[1] // LANES)]
        mc = functools.reduce(jnp.maximum, tiles)
        m_old = m_ref[...]
        m_new = jnp.maximum(m_old, jnp.max(mc, axis=-1, keepdims=True))
        alpha = jnp.exp2(m_old - m_new)
        pb = jnp.concatenate([jnp.exp2((t - m_new).astype(BF16)) for t in tiles], axis=1)
        pv = jnp.dot(pb, vb1, preferred_element_type=F32)
        a_ref[:, :LANES] = alpha * a_ref[:, :LANES] + pv[:, :LANES]
        a_ref[:, LANES:] = alpha * a_ref[:, LANES:] + pv[:, LANES:]
        m_ref[...] = m_new

    def block(j, width, diagonal):
        start = pl.multiple_of(j * tq, tq)
        kb = k_ref[0, pl.ds(start, width * tq), :]
        vb1 = jnp.concatenate([v_ref[0, pl.ds(start, width * tq), :],
                               jnp.ones((width * tq, LANES), BF16)], axis=1)
        s = _dot_nt(q12, kb)
        s1, s2 = s[:tq], s[tq:]
        if diagonal:
            rq = lax.broadcasted_iota(jnp.int32, (tq, width * tq), 0) // CHUNK
            ck = lax.broadcasted_iota(jnp.int32, (tq, width * tq), 1) // CHUNK
            mask = ck <= rq + (width - 1) * (tq // CHUNK)
            s1 = jnp.where(mask, s1, -jnp.inf)
            s2 = jnp.where(mask, s2, -jnp.inf)
        update(s1, vb1, m1_ref, a1_ref)
        update(s2, vb1, m2_ref, a2_ref)

    def wide_block(p, carry):
        block(ATTN_WIDE * p, ATTN_WIDE, False)
        return carry

    lax.fori_loop(0, i // ATTN_WIDE, wide_block, 0)
    for rem in range(ATTN_WIDE):
        @pl.when(i % ATTN_WIDE == rem)
        def _(rem=rem):
            block(i - rem, rem + 1, True)


    lam = lam_ref[0]
    o = a1_ref[:, :LANES] / a1_ref[:, LANES:] - lam * (a2_ref[:, :LANES] / a2_ref[:, LANES:])
    o = o * lax.rsqrt(jnp.mean(o * o, axis=-1, keepdims=True) + LN_EPS) * g_ref[...]
    o_ref[0] = (o * out_scale).astype(o_ref.dtype)


def _attention(lam, q, k, v, g, lambda_init, tq):
    b, s, _ = q.shape
    kern = functools.partial(_attn_kernel, tq=tq, out_scale=1.0 - lambda_init)
    kv_spec = pl.BlockSpec((1, s, 2 * A_HEAD_DIM), lambda bi, h, i: (bi, 0, h))
    qo_spec = pl.BlockSpec((1, tq, 2 * A_HEAD_DIM), lambda bi, h, i: (bi, i, h))
    vec = lambda n: pltpu.VMEM((tq, n), F32)
    return pl.pallas_call(
        kern,
        grid=(b, A_HEADS, s // tq),
        in_specs=[pl.BlockSpec(memory_space=pltpu.SMEM), qo_spec, kv_spec, kv_spec,
                  pl.BlockSpec((1, 2 * A_HEAD_DIM), lambda bi, h, i: (0, 0))],
        out_specs=qo_spec,
        out_shape=jax.ShapeDtypeStruct((b, s, D_MODEL), BF16),
        scratch_shapes=[vec(LANES), vec(2 * LANES)] * 2,
        compiler_params=_cparams(("arbitrary", "arbitrary", "arbitrary")),
        name="diff_attention",
    )(lam, q, k, v, g)


def _out_proj_kernel(o_ref, x_ref, w_ref, g_ref, b_ref, y_ref):
    h = jnp.dot(o_ref[...], w_ref[...], preferred_element_type=F32)
    y_ref[...] = _layer_norm(ALPHA * x_ref[...] + h, g_ref[...], b_ref[...])


def _out_proj(o2, x2, w, g, b, tm):
    t = x2.shape[0]
    row = lambda i: (i, 0)
    fixed = lambda i: (0, 0)
    return pl.pallas_call(
        _out_proj_kernel,
        grid=(t // tm,),
        in_specs=[pl.BlockSpec((tm, D_MODEL), row), pl.BlockSpec((tm, D_MODEL), row),
                  pl.BlockSpec((D_MODEL, D_MODEL), fixed),
                  pl.BlockSpec((1, D_MODEL), fixed), pl.BlockSpec((1, D_MODEL), fixed)],
        out_specs=pl.BlockSpec((tm, D_MODEL), row),
        out_shape=jax.ShapeDtypeStruct((t, D_MODEL), F32),
        compiler_params=_cparams(("arbitrary",)),
        name="out_proj_ln",
    )(o2, x2, w, g, b)


def _rotary_tables(seq):
    pos = jnp.arange(seq, dtype=F32)
    inv_freq = ROPE_THETA ** (-jnp.arange(0, ROT_DIM, 2, dtype=F32) / ROT_DIM)
    ang = pos[:, None] * inv_freq[None, :]
    cos, sin = jnp.cos(ang), jnp.sin(ang)
    half = ROT_DIM // 2
    pad = A_HEAD_DIM - ROT_DIM
    one = jnp.ones((seq, pad), F32)
    zero = jnp.zeros((seq, pad), F32)
    zh = jnp.zeros((seq, half), F32)
    cf = jnp.concatenate([cos, cos, one], axis=1)
    s1 = jnp.concatenate([-sin, zh, zero], axis=1)
    s2 = jnp.concatenate([zh, sin, zero], axis=1)
    rep = LANES // A_HEAD_DIM
    return tuple(jnp.tile(t, (1, rep)) for t in (cf, s1, s2))


def _attn_layer(x2, bsz, seq, w_in, w_out, lam_params, subln, lambda_init, ln_g, ln_b, tables,
                tm=512, tq=512):
    q, k, v = _attn_proj(x2, w_in.astype(BF16), *tables, seq, tm)
    lp = lam_params.astype(F32)
    lam = jnp.exp(jnp.sum(lp[0] * lp[1])) - jnp.exp(jnp.sum(lp[2] * lp[3])) + lambda_init
    shp = (bsz, seq, D_MODEL)
    o = _attention(lam.reshape(1), q.reshape(shp), k.reshape(shp), v.reshape(shp),
                   subln.reshape(1, -1).astype(F32), lambda_init, tq)
    return _out_proj(o.reshape(bsz * seq, D_MODEL), x2, w_out.astype(BF16),
                     ln_g.reshape(1, -1), ln_b.reshape(1, -1), tm)


H_CHUNK = 128
H_SUB = 16


def _hgrn_proj_kernel(x_ref, w_ref, lb_ref, q_ref, k_ref, v_ref, lf_ref, g_ref):
    xb = x_ref[...].astype(BF16)
    w = H_HEADS * H_EXPAND
    q = jnp.dot(xb, w_ref[:, 0:w], preferred_element_type=F32)
    q_ref[...] = q * jax.nn.sigmoid(q)
    f = jnp.dot(xb, w_ref[:, w:2 * w], preferred_element_type=F32)
    lb = lb_ref[...]
    forget = lb + (1.0 - lb) * jax.nn.sigmoid(f)
    lf_ref[...] = jnp.log(forget)
    k_ref[...] = 1.0 - forget
    v_ref[...] = jnp.dot(xb, w_ref[:, 2 * w:3 * w], preferred_element_type=F32)
    g = jnp.dot(xb, w_ref[:, 3 * w:4 * w], preferred_element_type=F32)
    g_ref[...] = g * jax.nn.sigmoid(g)


def _hgrn_proj(x2, w, lb, tm):
    t = x2.shape[0]
    row = lambda i: (i, 0)
    fixed = lambda i: (0, 0)
    out = jax.ShapeDtypeStruct((t, D_MODEL), F32)
    return pl.pallas_call(
        _hgrn_proj_kernel,
        grid=(t // tm,),
        in_specs=[pl.BlockSpec((tm, D_MODEL), row), pl.BlockSpec((D_MODEL, 4 * D_MODEL), fixed),
                  pl.BlockSpec((1, D_MODEL), fixed)],
        out_specs=[pl.BlockSpec((tm, D_MODEL), row)] * 5,
        out_shape=[out] * 5,
        compiler_params=_cparams(("arbitrary",)),
        name="hgrn_proj",
    )(x2, w, lb)


H_FAST_SPREAD = 96.0


def _hgrn_cumsum(lf):
    c = H_CHUNK
    r_i = lax.broadcasted_iota(jnp.int32, (c, c), 0)
    c_i = lax.broadcasted_iota(jnp.int32, (c, c), 1)
    tri = (r_i >= c_i).astype(BF16)
    hi = lf.astype(BF16)
    lo = (lf - hi.astype(F32)).astype(BF16)
    b2 = jnp.dot(tri, jnp.concatenate([hi, lo], axis=1), preferred_element_type=F32)
    return (b2[:, :H_EXPAND] + b2[:, H_EXPAND:]) * math.log2(math.e)


def _hgrn_spread(b):
    nb = H_CHUNK // H_SUB
    ends = [b[(j + 1) * H_SUB - 1:(j + 1) * H_SUB, :] for j in range(nb)]
    drops = [-ends[0]] + [ends[j - 1] - ends[j] for j in range(1, nb)]
    return functools.reduce(jnp.maximum, drops)


def _hgrn_chunk(q, k, v, b, st, factored):
    c, nb = H_CHUNK, H_CHUNK // H_SUB
    r_i = lax.broadcasted_iota(jnp.int32, (c, c), 0)
    c_i = lax.broadcasted_iota(jnp.int32, (c, c), 1)
    b_last = b[c - 1:c, :]

    o = _dot_nt((q * jnp.exp2(b)).astype(BF16), st.astype(BF16))
    ke = k * jnp.exp2(b_last - b)
    st_new = st * jnp.exp2(b_last) + _dot_tn(v.astype(BF16), ke.astype(BF16))

    ends = [b[(j + 1) * H_SUB - 1:(j + 1) * H_SUB, :] for j in range(nb)]
    e_blk = jnp.concatenate([jnp.broadcast_to(e, (H_SUB, H_EXPAND)) for e in ends], axis=0)
    kt = k * jnp.exp2(e_blk - b)
    row_blk = lax.broadcasted_iota(jnp.int32, (c, H_EXPAND), 0) // H_SUB
    q_big = jnp.concatenate(
        [(q * jnp.exp2(jnp.minimum(b - ends[j], 0.0))).astype(BF16) for j in range(nb)], axis=1)
    k_big = jnp.concatenate(
        [jnp.where(row_blk == j, kt, 0.0).astype(BF16) for j in range(nb)], axis=1)
    a = _dot_nt(q_big, k_big)
    a = jnp.where(r_i // H_SUB > c_i // H_SUB, a, 0.0)

    if factored:
        a_d = _dot_nt((q * jnp.exp2(b - e_blk)).astype(BF16), kt.astype(BF16))
        a = a + jnp.where((r_i // H_SUB == c_i // H_SUB) & (r_i >= c_i), a_d, 0.0)
        return o + jnp.dot(a.astype(BF16), v.astype(BF16), preferred_element_type=F32), st_new

    o = o + jnp.dot(a.astype(BF16), v.astype(BF16), preferred_element_type=F32)
    t_idx = lax.broadcasted_iota(jnp.int32, (H_SUB, 1), 0)
    diag = []
    for j in range(nb):
        sl = slice(j * H_SUB, (j + 1) * H_SUB)
        qb, kb, vb, bb = q[sl], k[sl], v[sl], b[sl]
        od = jnp.zeros((H_SUB, H_EXPAND), F32)
        for s in range(H_SUB):
            e = jnp.exp2(bb - bb[s:s + 1, :])
            col = jnp.sum(qb * (kb[s:s + 1, :] * e), axis=-1, keepdims=True)
            od = od + jnp.where(t_idx >= s, col, 0.0) * vb[s:s + 1, :]
        diag.append(od)
    return o + jnp.concatenate(diag, axis=0), st_new


def _hgrn_kernel(q_ref, k_ref, v_ref, lf_ref, g_ref, ng_ref, o_ref, st_ref, b_ref, *, n_chunks):
    @pl.when(pl.program_id(2) == 0)
    def _():
        st_ref[...] = jnp.zeros(st_ref.shape, F32)

    spread = None
    for ci in range(n_chunks):
        sl = slice(ci * H_CHUNK, (ci + 1) * H_CHUNK)
        b = _hgrn_cumsum(lf_ref[sl, :])
        b_ref[sl, :] = b
        sp = _hgrn_spread(b)
        spread = sp if spread is None else jnp.maximum(spread, sp)
    mild = jnp.max(spread) < H_FAST_SPREAD

    def run(factored):
        st = st_ref[...]
        for ci in range(n_chunks):
            sl = slice(ci * H_CHUNK, (ci + 1) * H_CHUNK)
            o, st = _hgrn_chunk(q_ref[sl, :], k_ref[sl, :], v_ref[sl, :], b_ref[sl, :], st, factored)
            o = o * lax.rsqrt(jnp.mean(o * o, axis=-1, keepdims=True) + LN_EPS) * ng_ref[...]
            o_ref[sl, :] = (o * g_ref[sl, :]).astype(o_ref.dtype)
        st_ref[...] = st

    @pl.when(mild)
    def _():
        run(True)

    @pl.when(jnp.logical_not(mild))
    def _():
        run(False)


def _hgrn_recurrence(q, k, v, lf, g, ng, bsz, seq, tc):
    t = q.shape[0]
    nt = seq // tc
    blk = pl.BlockSpec((tc, H_EXPAND), lambda b, h, c: (b * nt + c, h))
    return pl.pallas_call(
        functools.partial(_hgrn_kernel, n_chunks=tc // H_CHUNK),
        grid=(bsz, H_HEADS, nt),
        in_specs=[blk] * 5 + [pl.BlockSpec((1, H_EXPAND), lambda b, h, c: (0, 0))],
        out_specs=blk,
        out_shape=jax.ShapeDtypeStruct((t, D_MODEL), BF16),
        scratch_shapes=[pltpu.VMEM((H_EXPAND, H_EXPAND), F32), pltpu.VMEM((tc, H_EXPAND), F32)],
        compiler_params=_cparams(("arbitrary", "arbitrary", "arbitrary")),
        name="hgrn_recurrence",
    )(q, k, v, lf, g, ng)


def _hgrn_layer(x2, bsz, seq, w_in, w_out, lb, norm_g, ln_g, ln_b, tm=256, tc=2048):
    q, k, v, lf, g = _hgrn_proj(x2, w_in.astype(BF16), lb.reshape(1, -1).astype(F32), tm)
    o = _hgrn_recurrence(q, k, v, lf, g, norm_g.reshape(1, -1).astype(F32), bsz, seq, tc)
    return _out_proj(o, x2, w_out.astype(BF16), ln_g.reshape(1, -1), ln_b.reshape(1, -1), 2 * tm)


GROUP_SIZE = N_EXPERTS // N_GROUPS
X_WORDS = D_MODEL // 2
X_SLAB = X_WORDS // LANES


def _router_kernel(x_ref, wt_ref, bias_ref, idx_ref, rank_ref, gate_tm_ref, cnt_ref,
                   carry_ref, *, tr):
    @pl.when(pl.program_id(0) == 0)
    def _():
        carry_ref[...] = jnp.zeros(carry_ref.shape, F32)

    e_n = N_EXPERTS
    logits = lax.dot_general(wt_ref[...], x_ref[...], (((1,), (1,)), ((), ())),
                             precision=lax.Precision.HIGHEST, preferred_element_type=F32)
    scores = jax.nn.sigmoid(logits)
    choice = scores + bias_ref[...]

    ch3 = choice.reshape(N_GROUPS, GROUP_SIZE, tr)
    sub = lax.broadcasted_iota(jnp.int32, ch3.shape, 1)
    m1 = jnp.max(ch3, axis=1, keepdims=True)
    first = jnp.min(jnp.where(ch3 == m1, sub, GROUP_SIZE), axis=1, keepdims=True)
    m2 = jnp.max(jnp.where(sub == first, -jnp.inf, ch3), axis=1, keepdims=True)
    gs = (m1 + m2).reshape(N_GROUPS, tr)

    g_i = lax.broadcasted_iota(jnp.int32, gs.shape, 0)
    g_rank = jnp.zeros(gs.shape, F32)
    for g in range(N_GROUPS):
        row = gs[g:g + 1, :]
        ahead = (row > gs) | ((row == gs) & (g_i > g))
        g_rank = g_rank + jnp.where(ahead, 1.0, 0.0)
    g_keep = (g_rank < TOPK_GROUP).astype(F32).reshape(N_GROUPS, 1, tr)
    keep = jnp.broadcast_to(g_keep, (N_GROUPS, GROUP_SIZE, tr)).reshape(e_n, tr) > 0.5
    cm = jnp.where(keep, choice, -jnp.inf)

    e_i = lax.broadcasted_iota(jnp.int32, cm.shape, 0)
    sub_i = lax.broadcasted_iota(jnp.int32, (GROUP_SIZE, tr), 0)
    cm_g = [cm[g * GROUP_SIZE:(g + 1) * GROUP_SIZE, :] for g in range(N_GROUPS)]
    ranks = [jnp.zeros((GROUP_SIZE, tr), F32) for _ in range(N_GROUPS)]
    for e in range(e_n):
        ge, se = divmod(e, GROUP_SIZE)
        row = jnp.broadcast_to(cm_g[ge][se:se + 1, :], (GROUP_SIZE, tr))
        for g in range(N_GROUPS):
            if g > ge:
                ahead = row >= cm_g[g]
            elif g < ge:
                ahead = row > cm_g[g]
            else:
                ahead = (row > cm_g[g]) | ((row == cm_g[g]) & (sub_i > se))
            ranks[g] = ranks[g] + jnp.where(ahead, 1.0, 0.0)
    sel = jnp.concatenate(ranks, axis=0) < TOP_K
    sel_f = sel.astype(F32)
    w = jnp.where(sel, scores, 0.0)
    gate = w / (jnp.sum(w, axis=0, keepdims=True) + 1e-20) * ROUTED_SCALE

    sel_b = sel_f.astype(BF16)
    t_r = lax.broadcasted_iota(jnp.int32, (tr, tr), 0)
    t_c = lax.broadcasted_iota(jnp.int32, (tr, tr), 1)
    before = jnp.dot(sel_b, (t_r < t_c).astype(BF16), preferred_element_type=F32)
    tok_rank = carry_ref[:, 0:1] + before
    carry_ref[...] = carry_ref[...] + jnp.sum(sel_f, axis=1, keepdims=True)
    cnt_ref[...] = carry_ref[...]

    x_r = lax.broadcasted_iota(jnp.int32, (e_n, e_n), 0)
    x_c = lax.broadcasted_iota(jnp.int32, (e_n, e_n), 1)
    slot = jnp.dot((x_c < x_r).astype(BF16), sel_b, preferred_element_type=F32)
    e_f = e_i.astype(F32)
    idx_rows, gate_rows, rank_rows = [], [], []
    for j in range(TOP_K):
        pick = jnp.where(sel & (slot == j), 1.0, 0.0)
        idx_rows.append(jnp.sum(pick * e_f, axis=0, keepdims=True))
        gate_rows.append(jnp.sum(pick * gate, axis=0, keepdims=True))
        rank_rows.append(jnp.sum(pick * tok_rank, axis=0, keepdims=True))
    idx_ref[...] = jnp.concatenate(idx_rows, axis=0).astype(jnp.int32)
    rank_ref[...] = jnp.concatenate(rank_rows, axis=0).astype(jnp.int32)
    gates = jnp.concatenate(gate_rows, axis=0)
    padded = jnp.concatenate([gates, jnp.zeros((LANES - TOP_K, tr), F32)], axis=0)
    gate_tm_ref[...] = padded.T


def _router(x2, wt, bias, tr):
    t = x2.shape[0]
    col = lambda i: (0, i)
    return pl.pallas_call(
        functools.partial(_router_kernel, tr=tr),
        grid=(t // tr,),
        in_specs=[pl.BlockSpec((tr, D_MODEL), lambda i: (i, 0)),
                  pl.BlockSpec((N_EXPERTS, D_MODEL), lambda i: (0, 0)),
                  pl.BlockSpec((N_EXPERTS, 1), lambda i: (0, 0))],
        out_specs=[pl.BlockSpec((TOP_K, tr), col),
                   pl.BlockSpec((TOP_K, tr), col), pl.BlockSpec((tr, LANES), lambda i: (i, 0)),
                   pl.BlockSpec((N_EXPERTS, LANES), lambda i: (0, 0))],
        out_shape=[jax.ShapeDtypeStruct((TOP_K, t), jnp.int32),
                   jax.ShapeDtypeStruct((TOP_K, t), jnp.int32),
                   jax.ShapeDtypeStruct((t, LANES), F32),
                   jax.ShapeDtypeStruct((N_EXPERTS, LANES), F32)],
        scratch_shapes=[pltpu.VMEM((N_EXPERTS, LANES), F32)],
        compiler_params=_cparams(("arbitrary",)),
        name="moe_router",
    )(x2, wt, bias)


ISSUE_UNROLL = 8


def _pack_rows(x):
    lo = pltpu.bitcast(x[:, :X_WORDS].astype(BF16).astype(F32), jnp.uint32)
    hi = pltpu.bitcast(x[:, X_WORDS:].astype(BF16).astype(F32), jnp.uint32)
    return (lo >> 16) | (hi & jnp.uint32(0xFFFF0000))


def _unpack_words(words):
    return (pltpu.bitcast(words << 16, F32), pltpu.bitcast(words & jnp.uint32(0xFFFF0000), F32))


def _slots_kernel(start_ref, idx_ref, rank_ref, dest_ref):
    idx = idx_ref[...]
    dest = rank_ref[...]
    for e in range(N_EXPERTS):
        dest = dest + jnp.where(idx == e, start_ref[e], 0)
    dest_ref[...] = dest


def _slots(pad_start, idx, rank, tl):
    t = idx.shape[1]
    col = pl.BlockSpec((TOP_K, tl), lambda i: (0, i))
    return pl.pallas_call(
        _slots_kernel,
        grid=(t // tl,),
        in_specs=[pl.BlockSpec(memory_space=pltpu.SMEM), col, col],
        out_specs=col,
        out_shape=jax.ShapeDtypeStruct(idx.shape, jnp.int32),
        compiler_params=_cparams(("arbitrary",)),
        name="moe_slots",
    )(pad_start, idx, rank)


def _dispatch_kernel(dest_ref, x_ref, zeros_ref, xs_ref, xp_ref, sem, *, td):
    del zeros_ref
    words = _pack_rows(x_ref[...])
    for j in range(X_SLAB):
        xp_ref[:, j, :] = words[:, j * LANES:(j + 1) * LANES]

    def issue(g, carry):
        for u in range(ISSUE_UNROLL):
            t = g * ISSUE_UNROLL + u
            for j in range(TOP_K):
                pltpu.make_async_copy(xp_ref.at[t], xs_ref.at[dest_ref[j, t]], sem).start(priority=j % 2)
        return carry

    lax.fori_loop(0, td // ISSUE_UNROLL, issue, 0)
    for j in range(TOP_K):
        pltpu.make_async_copy(xp_ref, xs_ref.at[pl.ds(0, td)], sem).wait()


def _dispatch(dest, x2, zeros, td):
    t = x2.shape[0]
    return pl.pallas_call(
        functools.partial(_dispatch_kernel, td=td),
        grid=(t // td,),
        in_specs=[pl.BlockSpec((TOP_K, td), lambda i: (0, i), memory_space=pltpu.SMEM),
                  pl.BlockSpec((td, D_MODEL), lambda i: (i, 0)),
                  pl.BlockSpec(memory_space=pl.ANY)],
        out_specs=pl.BlockSpec(memory_space=pl.ANY),
        out_shape=jax.ShapeDtypeStruct(zeros.shape, zeros.dtype),
        input_output_aliases={2: 0},
        scratch_shapes=[pltpu.VMEM((td, X_SLAB, LANES), jnp.uint32), pltpu.SemaphoreType.DMA],
        compiler_params=_cparams(("arbitrary",)),
        name="moe_dispatch",
    )(dest, x2, zeros)


def _unpack_rows(words):
    return jnp.concatenate(_unpack_words(words), axis=1).astype(BF16)


def _swiglu(xb, w_gu, w_d):
    h = jnp.dot(xb, w_gu, preferred_element_type=F32)
    act = h[:, :EXPERT_FF] * jax.nn.sigmoid(h[:, :EXPERT_FF]) * h[:, EXPERT_FF:]
    return jnp.dot(act.astype(BF16), w_d, preferred_element_type=F32)


def _expert_kernel(blk_ref, exp_ref, used_ref, xs_ref, wgu_ref, wd_ref, ys_ref, wgu_b, wd_b, *, te):
    del blk_ref
    b = pl.program_id(0)
    live = b < used_ref[0]

    @pl.when(jnp.logical_or(b == 0, exp_ref[b] != exp_ref[jnp.maximum(b - 1, 0)]))
    def _():
        wgu_b[...] = wgu_ref[0, 0].astype(BF16)
        wd_b[...] = wd_ref[0, 0].astype(BF16)

    @pl.when(live)
    def _():
        words = jnp.concatenate(
            [xs_ref[pl.ds(j, te, stride=X_SLAB), :] for j in range(X_SLAB)], axis=1)
        y = _pack_rows(_swiglu(_unpack_rows(words), wgu_b[...], wd_b[...]))
        for c in range(X_SLAB):
            ys_ref[pl.ds(c, te, stride=X_SLAB), :] = y[:, c * LANES:(c + 1) * LANES]

    @pl.when(jnp.logical_not(live))
    def _():
        ys_ref[...] = jnp.zeros(ys_ref.shape, ys_ref.dtype)


def _experts(blk, exp, used, xs2, w_gu, w_d, layer, te):
    n_blocks = blk.shape[0]
    n_rows = xs2.shape[0] // X_SLAB
    return pl.pallas_call(
        functools.partial(_expert_kernel, te=te),
        grid_spec=pltpu.PrefetchScalarGridSpec(
            num_scalar_prefetch=3,
            grid=(n_blocks,),
            in_specs=[pl.BlockSpec((te * X_SLAB, LANES), lambda b, blk, exp, used: (blk[b], 0)),
                      pl.BlockSpec((1, 1, D_MODEL, 2 * EXPERT_FF),
                                   lambda b, blk, exp, used: (layer, exp[b], 0, 0)),
                      pl.BlockSpec((1, 1, EXPERT_FF, D_MODEL),
                                   lambda b, blk, exp, used: (layer, exp[b], 0, 0))],
            out_specs=pl.BlockSpec((te * X_SLAB, LANES), lambda b, blk, exp, used: (b, 0)),
            scratch_shapes=[pltpu.VMEM((D_MODEL, 2 * EXPERT_FF), BF16),
                            pltpu.VMEM((EXPERT_FF, D_MODEL), BF16)],
        ),
        out_shape=jax.ShapeDtypeStruct((n_rows * X_SLAB, LANES), jnp.uint32),
        compiler_params=_cparams(("arbitrary",)),
        name="moe_experts",
    )(blk, exp, used, xs2, w_gu, w_d)


def _combine_kernel(dest_ref, dest_next_ref, gate_ref, x_ref, wgu_ref, wd_ref, g_ref, b_ref, ys_ref, ys2_ref,
                    o_ref, ybuf_ref, sem, *, tc):
    i = pl.program_id(0)
    last = pl.num_programs(0) - 1
    slot = i % 2
    rows = TOP_K * tc

    def row_copy(d_ref, j, t, into):
        dst = ybuf_ref.at[pl.ds((into * rows + j * tc + t) * X_SLAB, X_SLAB)]
        return pltpu.make_async_copy(ys_ref.at[d_ref[j, t]], dst, sem.at[into])

    def wait_tile(into):
        pltpu.make_async_copy(ys2_ref.at[pl.ds(0, rows * X_SLAB)],
                              ybuf_ref.at[pl.ds(into * rows * X_SLAB, rows * X_SLAB)], sem.at[into]).wait()

    @pl.when(i == 0)
    def _():
        def issue(t, carry):
            for j in range(TOP_K):
                row_copy(dest_ref, j, t, 0).start(priority=j % 2)
            return carry
        lax.fori_loop(0, tc, issue, 0)

    wait_tile(slot)
    x = x_ref[...]
    gate = gate_ref[...]
    acc_lo = [None] * X_SLAB
    acc_hi = [None] * X_SLAB
    per_phase = tc // TOP_K
    for j in range(TOP_K):
        for t in range(j * per_phase, (j + 1) * per_phase):
            for jj in range(TOP_K):
                row_copy(dest_next_ref, jj, t, 1 - slot).start(priority=jj % 2)
        g_j = gate[:, j:j + 1]
        for c in range(X_SLAB):
            start = (slot * rows + j * tc) * X_SLAB + c
            lo, hi = _unpack_words(ybuf_ref[pl.ds(start, tc, stride=X_SLAB), :])
            acc_lo[c] = g_j * lo if j == 0 else acc_lo[c] + g_j * lo
            acc_hi[c] = g_j * hi if j == 0 else acc_hi[c] + g_j * hi
    routed = jnp.concatenate(acc_lo + acc_hi, axis=1)
    acc = ALPHA * x + routed + _swiglu(x.astype(BF16), wgu_ref[...], wd_ref[...])
    o_ref[...] = _layer_norm(acc, g_ref[...], b_ref[...])

    @pl.when(i == last)
    def _():
        wait_tile(1 - slot)


def _combine(dest, gate_tm, x2, ys, ws_gu, ws_d, g, b, tc):
    t = x2.shape[0]
    n = t // tc
    fixed = lambda i: (0, 0)
    row = lambda i: (i, 0)
    smem = lambda f: pl.BlockSpec((TOP_K, tc), f, memory_space=pltpu.SMEM)
    return pl.pallas_call(
        functools.partial(_combine_kernel, tc=tc),
        grid=(n,),
        in_specs=[smem(lambda i: (0, i)), smem(lambda i: (0, jnp.minimum(i + 1, n - 1))),
                  pl.BlockSpec((tc, LANES), row), pl.BlockSpec((tc, D_MODEL), row),
                  pl.BlockSpec((D_MODEL, 2 * EXPERT_FF), fixed), pl.BlockSpec((EXPERT_FF, D_MODEL), fixed),
                  pl.BlockSpec((1, D_MODEL), fixed), pl.BlockSpec((1, D_MODEL), fixed),
                  pl.BlockSpec(memory_space=pl.ANY), pl.BlockSpec(memory_space=pl.ANY)],
        out_specs=pl.BlockSpec((tc, D_MODEL), row),
        out_shape=jax.ShapeDtypeStruct((t, D_MODEL), F32),
        scratch_shapes=[pltpu.VMEM((2 * TOP_K * tc * X_SLAB, LANES), jnp.uint32),
                        pltpu.SemaphoreType.DMA((2,))],
        compiler_params=_cparams(("arbitrary",)),
        name="moe_combine",
    )(dest, dest, gate_tm, x2, ws_gu, ws_d, g, b, ys.reshape(-1, X_SLAB, LANES), ys)


def _moe_layer(x2, w_router, bias, w_gu, w_d, layer, ws_gu, ws_d, ln_g, ln_b, sorted_buf=None,
               te=1024, tr=512, td=512, tc=128):
    t = x2.shape[0]
    n_blocks = t * TOP_K // te + N_EXPERTS
    n_rows = n_blocks * te
    if sorted_buf is None:
        sorted_buf = jnp.zeros((n_rows, X_SLAB, LANES), jnp.uint32)
    idx, rank, gate_tm, cnt = _router(x2, w_router.T.astype(F32), bias.reshape(-1, 1).astype(F32), tr)

    counts = cnt[:, 0].astype(jnp.int32)
    padded = (counts + te - 1) // te * te
    pad_end = jnp.cumsum(padded)
    pad_start = pad_end - padded
    dest = _slots(pad_start, idx, rank, min(t, 4096))
    used = pad_end[-1] // te
    blk = jnp.minimum(jnp.arange(n_blocks, dtype=jnp.int32), used - 1)
    exp = jnp.sum((pad_end[None, :] // te <= blk[:, None]).astype(jnp.int32), axis=1)
    exp = jnp.minimum(exp, N_EXPERTS - 1)

    xs = _dispatch(dest, x2, sorted_buf, td)
    ys = _experts(blk, exp.astype(jnp.int32), used.reshape(1), xs.reshape(n_rows * X_SLAB, LANES),
                  w_gu, w_d, layer, te)
    out = _combine(dest, gate_tm, x2, ys, ws_gu.astype(BF16), ws_d.astype(BF16),
                   ln_g.reshape(1, -1), ln_b.reshape(1, -1), tc)
    return out, xs


def kernel(x, attn_w_in, attn_w_out, attn_lambda, attn_subln, hgrn_w_in, hgrn_w_out, hgrn_lower_bounds, hgrn_norm, moe_router, moe_router_bias, moe_w_gate_up, moe_w_down, shared_w_gate_up, shared_w_down, ln_gain, ln_bias):
    bsz, seq, _ = x.shape
    tables = _rotary_tables(seq)
    lb = jax.nn.softmax(hgrn_lower_bounds.astype(F32), axis=0)
    lb = jnp.cumsum(lb, axis=0) - lb[0]
    x2 = x.reshape(bsz * seq, D_MODEL)
    sorted_buf = None
    for layer in range(DEPTH):
        j = layer // 2
        if layer % 2 == 0:
            lambda_init = 0.8 - 0.6 * math.exp(-0.3 * layer)
            x2 = _attn_layer(x2, bsz, seq, attn_w_in[j], attn_w_out[j], attn_lambda[j], attn_subln[j],
                             lambda_init, ln_gain[layer, 0], ln_bias[layer, 0], tables)
        else:
            x2 = _hgrn_layer(x2, bsz, seq, hgrn_w_in[j], hgrn_w_out[j], lb[layer], hgrn_norm[j],
                             ln_gain[layer, 0], ln_bias[layer, 0])
        x2, sorted_buf = _moe_layer(x2, moe_router[layer], moe_router_bias[layer], moe_w_gate_up,
                                    moe_w_down, layer, shared_w_gate_up[layer], shared_w_down[layer],
                                    ln_gain[layer, 1], ln_bias[layer, 1], sorted_buf)
    return x2.reshape(bsz, seq, D_MODEL)
```

```python
import functools
import math

import jax
import jax.numpy as jnp
from jax import lax
from jax.experimental import pallas as pl
from jax.experimental.pallas import tpu as pltpu

F32 = jnp.float32
BF16 = jnp.bfloat16

D_MODEL = 1024
DEPTH = 4
CHUNK = 64
A_HEADS = 8
A_HEAD_DIM = 64
ROT_DIM = 16
ROPE_THETA = 500000.0
H_HEADS = 8
H_EXPAND = 128
N_EXPERTS = 64
N_GROUPS = 8
TOPK_GROUP = 4
TOP_K = 8
EXPERT_FF = 256
ROUTED_SCALE = 2.5
ALPHA = (2 * DEPTH) ** 0.25
LN_EPS = 1e-5
LANES = 128
VMEM_LIMIT = 48 * 1024 * 1024


def _cparams(sem):
    return pltpu.CompilerParams(dimension_semantics=sem, vmem_limit_bytes=VMEM_LIMIT)


def _layer_norm(y, g, b):
    mu = jnp.mean(y, axis=-1, keepdims=True)
    yc = y - mu
    var = jnp.mean(yc * yc, axis=-1, keepdims=True)
    return yc * lax.rsqrt(var + LN_EPS) * g + b


def _dot_nt(a, b):
    return lax.dot_general(a, b, (((1,), (1,)), ((), ())), preferred_element_type=F32)


def _dot_tn(a, b):
    return lax.dot_general(a, b, (((0,), (0,)), ((), ())), preferred_element_type=F32)


def _attn_proj_kernel(x_ref, w_ref, cf_ref, s1_ref, s2_ref, q_ref, k_ref, v_ref):
    xb = x_ref[...].astype(BF16)
    cf, s1, s2 = cf_ref[...], s1_ref[...], s2_ref[...]

    def rotary(y):
        outs = []
        for c in range(D_MODEL // LANES):
            yc = y[:, c * LANES:(c + 1) * LANES]
            outs.append(yc * cf + pltpu.roll(yc, LANES - ROT_DIM // 2, 1) * s1
                        + pltpu.roll(yc, ROT_DIM // 2, 1) * s2)
        return jnp.concatenate(outs, axis=1)

    q = jnp.dot(xb, w_ref[:, 0:D_MODEL], preferred_element_type=F32)
    q_ref[...] = (rotary(q) * (A_HEAD_DIM ** -0.5 * math.log2(math.e))).astype(BF16)
    k = jnp.dot(xb, w_ref[:, D_MODEL:2 * D_MODEL], preferred_element_type=F32)
    k_ref[...] = rotary(k).astype(BF16)
    v = jnp.dot(xb, w_ref[:, 2 * D_MODEL:3 * D_MODEL], preferred_element_type=F32)
    v_ref[...] = v.astype(BF16)


def _attn_proj(x2, w, cf, s1, s2, seq, tm):
    t = x2.shape[0]
    nseq = seq // tm
    row = lambda i: (i, 0)
    tab = lambda i: (i % nseq, 0)
    out = jax.ShapeDtypeStruct((t, D_MODEL), BF16)
    return pl.pallas_call(
        _attn_proj_kernel,
        grid=(t // tm,),
        in_specs=[pl.BlockSpec((tm, D_MODEL), row),
                  pl.BlockSpec((D_MODEL, 3 * D_MODEL), lambda i: (0, 0)),
                  pl.BlockSpec((tm, LANES), tab), pl.BlockSpec((tm, LANES), tab),
                  pl.BlockSpec((tm, LANES), tab)],
        out_specs=[pl.BlockSpec((tm, D_MODEL), row)] * 3,
        out_shape=[out, out, out],
        compiler_params=_cparams(("arbitrary",)),
        name="attn_proj",
    )(x2, w, cf, s1, s2)


ATTN_WIDE = 6


def _attn_kernel(lam_ref, q_ref, k_ref, v_ref, g_ref, o_ref, m1_ref, a1_ref, m2_ref, a2_ref,
                 *, tq, out_scale):
    i = pl.program_id(2)
    q = q_ref[0]
    lane = lax.broadcasted_iota(jnp.int32, q.shape, 1)
    zero = jnp.zeros_like(q)
    q1 = jnp.where(lane < A_HEAD_DIM, q, zero)
    q2 = jnp.where(lane >= A_HEAD_DIM, q, zero)
    q12 = jnp.concatenate([q1, q2], axis=0)

    for m_ref, a_ref in ((m1_ref, a1_ref), (m2_ref, a2_ref)):
        m_ref[...] = jnp.full(m_ref.shape, -jnp.inf, F32)
        a_ref[...] = jnp.zeros(a_ref.shape, F32)

    def update(s, vb1, m_ref, a_ref):
        tiles = [s[:, c * LANES:(c + 1) * LANES] for c in range(s.shape[1] // LANES)]
        mc = functools.reduce(jnp.maximum, tiles)
        m_old = m_ref[...]
        m_new = jnp.maximum(m_old, jnp.max(mc, axis=-1, keepdims=True))
        alpha = jnp.exp2(m_old - m_new)
        pb = jnp.concatenate([jnp.exp2((t - m_new).astype(BF16)) for t in tiles], axis=1)
        pv = jnp.dot(pb, vb1, preferred_element_type=F32)
        a_ref[:, :LANES] = alpha * a_ref[:, :LANES] + pv[:, :LANES]
        a_ref[:, LANES:] = alpha * a_ref[:, LANES:] + pv[:, LANES:]
        m_ref[...] = m_new

    def block(j, width, diagonal):
        start = pl.multiple_of(j * tq, tq)
        kb = k_ref[0, pl.ds(start, width * tq), :]
        vb1 = jnp.concatenate([v_ref[0, pl.ds(start, width * tq), :],
                               jnp.ones((width * tq, LANES), BF16)], axis=1)
        s = _dot_nt(q12, kb)
        s1, s2 = s[:tq], s[tq:]
        if diagonal:
            rq = lax.broadcasted_iota(jnp.int32, (tq, width * tq), 0) // CHUNK
            ck = lax.broadcasted_iota(jnp.int32, (tq, width * tq), 1) // CHUNK
            mask = ck <= rq + (width - 1) * (tq // CHUNK)
            s1 = jnp.where(mask, s1, -jnp.inf)
            s2 = jnp.where(mask, s2, -jnp.inf)
        update(s1, vb1, m1_ref, a1_ref)
        update(s2, vb1, m2_ref, a2_ref)

    def wide_block(p, carry):
        block(ATTN_WIDE * p, ATTN_WIDE, False)
        return carry

    lax.fori_loop(0, i // ATTN_WIDE, wide_block, 0)
    for rem in range(ATTN_WIDE):
        @pl.when(i % ATTN_WIDE == rem)
        def _(rem=rem):
            block(i - rem, rem + 1, True)


    lam = lam_ref[0]
    o = a1_ref[:, :LANES] / a1_ref[:, LANES:] - lam * (a2_ref[:, :LANES] / a2_ref[:, LANES:])
    o = o * lax.rsqrt(jnp.mean(o * o, axis=-1, keepdims=True) + LN_EPS) * g_ref[...]
    o_ref[0] = (o * out_scale).astype(o_ref.dtype)


def _attention(lam, q, k, v, g, lambda_init, tq):
    b, s, _ = q.shape
    kern = functools.partial(_attn_kernel, tq=tq, out_scale=1.0 - lambda_init)
    kv_spec = pl.BlockSpec((1, s, 2 * A_HEAD_DIM), lambda bi, h, i: (bi, 0, h))
    qo_spec = pl.BlockSpec((1, tq, 2 * A_HEAD_DIM), lambda bi, h, i: (bi, i, h))
    vec = lambda n: pltpu.VMEM((tq, n), F32)
    return pl.pallas_call(
        kern,
        grid=(b, A_HEADS, s // tq),
        in_specs=[pl.BlockSpec(memory_space=pltpu.SMEM), qo_spec, kv_spec, kv_spec,
                  pl.BlockSpec((1, 2 * A_HEAD_DIM), lambda bi, h, i: (0, 0))],
        out_specs=qo_spec,
        out_shape=jax.ShapeDtypeStruct((b, s, D_MODEL), BF16),
        scratch_shapes=[vec(LANES), vec(2 * LANES)] * 2,
        compiler_params=_cparams(("arbitrary", "arbitrary", "arbitrary")),
        name="diff_attention",
    )(lam, q, k, v, g)


def _out_proj_kernel(o_ref, x_ref, w_ref, g_ref, b_ref, y_ref):
    h = jnp.dot(o_ref[...], w_ref[...], preferred_element_type=F32)
    y_ref[...] = _layer_norm(ALPHA * x_ref[...] + h, g_ref[...], b_ref[...])


def _out_proj(o2, x2, w, g, b, tm):
    t = x2.shape[0]
    row = lambda i: (i, 0)
    fixed = lambda i: (0, 0)
    return pl.pallas_call(
        _out_proj_kernel,
        grid=(t // tm,),
        in_specs=[pl.BlockSpec((tm, D_MODEL), row), pl.BlockSpec((tm, D_MODEL), row),
                  pl.BlockSpec((D_MODEL, D_MODEL), fixed),
                  pl.BlockSpec((1, D_MODEL), fixed), pl.BlockSpec((1, D_MODEL), fixed)],
        out_specs=pl.BlockSpec((tm, D_MODEL), row),
        out_shape=jax.ShapeDtypeStruct((t, D_MODEL), F32),
        compiler_params=_cparams(("arbitrary",)),
        name="out_proj_ln",
    )(o2, x2, w, g, b)


def _rotary_tables(seq):
    pos = jnp.arange(seq, dtype=F32)
    inv_freq = ROPE_THETA ** (-jnp.arange(0, ROT_DIM, 2, dtype=F32) / ROT_DIM)
    ang = pos[:, None] * inv_freq[None, :]
    cos, sin = jnp.cos(ang), jnp.sin(ang)
    half = ROT_DIM // 2
    pad = A_HEAD_DIM - ROT_DIM
    one = jnp.ones((seq, pad), F32)
    zero = jnp.zeros((seq, pad), F32)
    zh = jnp.zeros((seq, half), F32)
    cf = jnp.concatenate([cos, cos, one], axis=1)
    s1 = jnp.concatenate([-sin, zh, zero], axis=1)
    s2 = jnp.concatenate([zh, sin, zero], axis=1)
    rep = LANES // A_HEAD_DIM
    return tuple(jnp.tile(t, (1, rep)) for t in (cf, s1, s2))


def _attn_layer(x2, bsz, seq, w_in, w_out, lam_params, subln, lambda_init, ln_g, ln_b, tables,
                tm=512, tq=512):
    q, k, v = _attn_proj(x2, w_in.astype(BF16), *tables, seq, tm)
    lp = lam_params.astype(F32)
    lam = jnp.exp(jnp.sum(lp[0] * lp[1])) - jnp.exp(jnp.sum(lp[2] * lp[3])) + lambda_init
    shp = (bsz, seq, D_MODEL)
    o = _attention(lam.reshape(1), q.reshape(shp), k.reshape(shp), v.reshape(shp),
                   subln.reshape(1, -1).astype(F32), lambda_init, tq)
    return _out_proj(o.reshape(bsz * seq, D_MODEL), x2, w_out.astype(BF16),
                     ln_g.reshape(1, -1), ln_b.reshape(1, -1), tm)


H_CHUNK = 128
H_SUB = 16


def _hgrn_proj_kernel(x_ref, w_ref, lb_ref, q_ref, k_ref, v_ref, lf_ref, g_ref):
    xb = x_ref[...].astype(BF16)
    w = H_HEADS * H_EXPAND
    q = jnp.dot(xb, w_ref[:, 0:w], preferred_element_type=F32)
    q_ref[...] = q * jax.nn.sigmoid(q)
    f = jnp.dot(xb, w_ref[:, w:2 * w], preferred_element_type=F32)
    lb = lb_ref[...]
    forget = lb + (1.0 - lb) * jax.nn.sigmoid(f)
    lf_ref[...] = jnp.log(forget)
    k_ref[...] = 1.0 - forget
    v_ref[...] = jnp.dot(xb, w_ref[:, 2 * w:3 * w], preferred_element_type=F32)
    g = jnp.dot(xb, w_ref[:, 3 * w:4 * w], preferred_element_type=F32)
    g_ref[...] = g * jax.nn.sigmoid(g)


def _hgrn_proj(x2, w, lb, tm):
    t = x2.shape[0]
    row = lambda i: (i, 0)
    fixed = lambda i: (0, 0)
    out = jax.ShapeDtypeStruct((t, D_MODEL), F32)
    return pl.pallas_call(
        _hgrn_proj_kernel,
        grid=(t // tm,),
        in_specs=[pl.BlockSpec((tm, D_MODEL), row), pl.BlockSpec((D_MODEL, 4 * D_MODEL), fixed),
                  pl.BlockSpec((1, D_MODEL), fixed)],
        out_specs=[pl.BlockSpec((tm, D_MODEL), row)] * 5,
        out_shape=[out] * 5,
        compiler_params=_cparams(("arbitrary",)),
        name="hgrn_proj",
    )(x2, w, lb)


H_FAST_SPREAD = 96.0


def _hgrn_cumsum(lf):
    c = H_CHUNK
    r_i = lax.broadcasted_iota(jnp.int32, (c, c), 0)
    c_i = lax.broadcasted_iota(jnp.int32, (c, c), 1)
    tri = (r_i >= c_i).astype(BF16)
    hi = lf.astype(BF16)
    lo = (lf - hi.astype(F32)).astype(BF16)
    b2 = jnp.dot(tri, jnp.concatenate([hi, lo], axis=1), preferred_element_type=F32)
    return (b2[:, :H_EXPAND] + b2[:, H_EXPAND:]) * math.log2(math.e)


def _hgrn_spread(b):
    nb = H_CHUNK // H_SUB
    ends = [b[(j + 1) * H_SUB - 1:(j + 1) * H_SUB, :] for j in range(nb)]
    drops = [-ends[0]] + [ends[j - 1] - ends[j] for j in range(1, nb)]
    return functools.reduce(jnp.maximum, drops)


def _hgrn_chunk(q, k, v, b, st, factored):
    c, nb = H_CHUNK, H_CHUNK // H_SUB
    r_i = lax.broadcasted_iota(jnp.int32, (c, c), 0)
    c_i = lax.broadcasted_iota(jnp.int32, (c, c), 1)
    b_last = b[c - 1:c, :]

    o = _dot_nt((q * jnp.exp2(b)).astype(BF16), st.astype(BF16))
    ke = k * jnp.exp2(b_last - b)
    st_new = st * jnp.exp2(b_last) + _dot_tn(v.astype(BF16), ke.astype(BF16))

    ends = [b[(j + 1) * H_SUB - 1:(j + 1) * H_SUB, :] for j in range(nb)]
    e_blk = jnp.concatenate([jnp.broadcast_to(e, (H_SUB, H_EXPAND)) for e in ends], axis=0)
    kt = k * jnp.exp2(e_blk - b)
    row_blk = lax.broadcasted_iota(jnp.int32, (c, H_EXPAND), 0) // H_SUB
    q_big = jnp.concatenate(
        [(q * jnp.exp2(jnp.minimum(b - ends[j], 0.0))).astype(BF16) for j in range(nb)], axis=1)
    k_big = jnp.concatenate(
        [jnp.where(row_blk == j, kt, 0.0).astype(BF16) for j in range(nb)], axis=1)
    a = _dot_nt(q_big, k_big)
    a = jnp.where(r_i // H_SUB > c_i // H_SUB, a, 0.0)

    if factored:
        a_d = _dot_nt((q * jnp.exp2(b - e_blk)).astype(BF16), kt.astype(BF16))
        a = a + jnp.where((r_i // H_SUB == c_i // H_SUB) & (r_i >= c_i), a_d, 0.0)
        return o + jnp.dot(a.astype(BF16), v.astype(BF16), preferred_element_type=F32), st_new

    o = o + jnp.dot(a.astype(BF16), v.astype(BF16), preferred_element_type=F32)
    t_idx = lax.broadcasted_iota(jnp.int32, (H_SUB, 1), 0)
    diag = []
    for j in range(nb):
        sl = slice(j * H_SUB, (j + 1) * H_SUB)
        qb, kb, vb, bb = q[sl], k[sl], v[sl], b[sl]
        od = jnp.zeros((H_SUB, H_EXPAND), F32)
        for s in range(H_SUB):
            e = jnp.exp2(bb - bb[s:s + 1, :])
            col = jnp.sum(qb * (kb[s:s + 1, :] * e), axis=-1, keepdims=True)
            od = od + jnp.where(t_idx >= s, col, 0.0) * vb[s:s + 1, :]
        diag.append(od)
    return o + jnp.concatenate(diag, axis=0), st_new


def _hgrn_kernel(q_ref, k_ref, v_ref, lf_ref, g_ref, ng_ref, o_ref, st_ref, b_ref, *, n_chunks):
    @pl.when(pl.program_id(2) == 0)
    def _():
        st_ref[...] = jnp.zeros(st_ref.shape, F32)

    spread = None
    for ci in range(n_chunks):
        sl = slice(ci * H_CHUNK, (ci + 1) * H_CHUNK)
        b = _hgrn_cumsum(lf_ref[sl, :])
        b_ref[sl, :] = b
        sp = _hgrn_spread(b)
        spread = sp if spread is None else jnp.maximum(spread, sp)
    mild = jnp.max(spread) < H_FAST_SPREAD

    def run(factored):
        st = st_ref[...]
        for ci in range(n_chunks):
            sl = slice(ci * H_CHUNK, (ci + 1) * H_CHUNK)
            o, st = _hgrn_chunk(q_ref[sl, :], k_ref[sl, :], v_ref[sl, :], b_ref[sl, :], st, factored)
            o = o * lax.rsqrt(jnp.mean(o * o, axis=-1, keepdims=True) + LN_EPS) * ng_ref[...]
            o_ref[sl, :] = (o * g_ref[sl, :]).astype(o_ref.dtype)
        st_ref[...] = st

    @pl.when(mild)
    def _():
        run(True)

    @pl.when(jnp.logical_not(mild))
    def _():
        run(False)


def _hgrn_recurrence(q, k, v, lf, g, ng, bsz, seq, tc):
    t = q.shape[0]
    nt = seq // tc
    blk = pl.BlockSpec((tc, H_EXPAND), lambda b, h, c: (b * nt + c, h))
    return pl.pallas_call(
        functools.partial(_hgrn_kernel, n_chunks=tc // H_CHUNK),
        grid=(bsz, H_HEADS, nt),
        in_specs=[blk] * 5 + [pl.BlockSpec((1, H_EXPAND), lambda b, h, c: (0, 0))],
        out_specs=blk,
        out_shape=jax.ShapeDtypeStruct((t, D_MODEL), BF16),
        scratch_shapes=[pltpu.VMEM((H_EXPAND, H_EXPAND), F32), pltpu.VMEM((tc, H_EXPAND), F32)],
        compiler_params=_cparams(("arbitrary", "arbitrary", "arbitrary")),
        name="hgrn_recurrence",
    )(q, k, v, lf, g, ng)


def _hgrn_layer(x2, bsz, seq, w_in, w_out, lb, norm_g, ln_g, ln_b, tm=256, tc=2048):
    q, k, v, lf, g = _hgrn_proj(x2, w_in.astype(BF16), lb.reshape(1, -1).astype(F32), tm)
    o = _hgrn_recurrence(q, k, v, lf, g, norm_g.reshape(1, -1).astype(F32), bsz, seq, tc)
    return _out_proj(o, x2, w_out.astype(BF16), ln_g.reshape(1, -1), ln_b.reshape(1, -1), 2 * tm)


GROUP_SIZE = N_EXPERTS // N_GROUPS
X_WORDS = D_MODEL // 2
X_SLAB = X_WORDS // LANES


def _router_kernel(x_ref, wt_ref, bias_ref, idx_ref, rank_ref, gate_tm_ref, cnt_ref,
                   carry_ref, *, tr):
    @pl.when(pl.program_id(0) == 0)
    def _():
        carry_ref[...] = jnp.zeros(carry_ref.shape, F32)

    e_n = N_EXPERTS
    logits = lax.dot_general(wt_ref[...], x_ref[...], (((1,), (1,)), ((), ())),
                             precision=lax.Precision.HIGHEST, preferred_element_type=F32)
    scores = jax.nn.sigmoid(logits)
    choice = scores + bias_ref[...]

    ch3 = choice.reshape(N_GROUPS, GROUP_SIZE, tr)
    sub = lax.broadcasted_iota(jnp.int32, ch3.shape, 1)
    m1 = jnp.max(ch3, axis=1, keepdims=True)
    first = jnp.min(jnp.where(ch3 == m1, sub, GROUP_SIZE), axis=1, keepdims=True)
    m2 = jnp.max(jnp.where(sub == first, -jnp.inf, ch3), axis=1, keepdims=True)
    gs = (m1 + m2).reshape(N_GROUPS, tr)

    g_i = lax.broadcasted_iota(jnp.int32, gs.shape, 0)
    g_rank = jnp.zeros(gs.shape, F32)
    for g in range(N_GROUPS):
        row = gs[g:g + 1, :]
        ahead = (row > gs) | ((row == gs) & (g_i > g))
        g_rank = g_rank + jnp.where(ahead, 1.0, 0.0)
    g_keep = (g_rank < TOPK_GROUP).astype(F32).reshape(N_GROUPS, 1, tr)
    keep = jnp.broadcast_to(g_keep, (N_GROUPS, GROUP_SIZE, tr)).reshape(e_n, tr) > 0.5
    cm = jnp.where(keep, choice, -jnp.inf)

    e_i = lax.broadcasted_iota(jnp.int32, cm.shape, 0)
    sel_f = jnp.zeros(cm.shape, F32)
    work = cm
    for _ in range(TOP_K):
        best = jnp.max(work, axis=0, keepdims=True)
        first = jnp.min(jnp.where(work == best, e_i, e_n), axis=0, keepdims=True)
        pick = e_i == first
        sel_f = jnp.where(pick, 1.0, sel_f)
        work = jnp.where(pick, -jnp.inf, work)
    sel = sel_f > 0.5
    w = jnp.where(sel, scores, 0.0)
    gate = w / (jnp.sum(w, axis=0, keepdims=True) + 1e-20) * ROUTED_SCALE

    sel_b = sel_f.astype(BF16)
    t_r = lax.broadcasted_iota(jnp.int32, (tr, tr), 0)
    t_c = lax.broadcasted_iota(jnp.int32, (tr, tr), 1)
    before = jnp.dot(sel_b, (t_r < t_c).astype(BF16), preferred_element_type=F32)
    tok_rank = carry_ref[:, 0:1] + before
    carry_ref[...] = carry_ref[...] + jnp.sum(sel_f, axis=1, keepdims=True)
    cnt_ref[...] = carry_ref[...]

    x_r = lax.broadcasted_iota(jnp.int32, (e_n, e_n), 0)
    x_c = lax.broadcasted_iota(jnp.int32, (e_n, e_n), 1)
    slot = jnp.dot((x_c < x_r).astype(BF16), sel_b, preferred_element_type=F32)
    e_f = e_i.astype(F32)
    idx_rows, gate_rows, rank_rows = [], [], []
    for j in range(TOP_K):
        pick = jnp.where(sel & (slot == j), 1.0, 0.0)
        idx_rows.append(jnp.sum(pick * e_f, axis=0, keepdims=True))
        gate_rows.append(jnp.sum(pick * gate, axis=0, keepdims=True))
        rank_rows.append(jnp.sum(pick * tok_rank, axis=0, keepdims=True))
    idx_ref[...] = jnp.concatenate(idx_rows, axis=0).astype(jnp.int32)
    rank_ref[...] = jnp.concatenate(rank_rows, axis=0).astype(jnp.int32)
    gates = jnp.concatenate(gate_rows, axis=0)
    padded = jnp.concatenate([gates, jnp.zeros((LANES - TOP_K, tr), F32)], axis=0)
    gate_tm_ref[...] = padded.T


def _router(x2, wt, bias, tr):
    t = x2.shape[0]
    col = lambda i: (0, i)
    return pl.pallas_call(
        functools.partial(_router_kernel, tr=tr),
        grid=(t // tr,),
        in_specs=[pl.BlockSpec((tr, D_MODEL), lambda i: (i, 0)),
                  pl.BlockSpec((N_EXPERTS, D_MODEL), lambda i: (0, 0)),
                  pl.BlockSpec((N_EXPERTS, 1), lambda i: (0, 0))],
        out_specs=[pl.BlockSpec((TOP_K, tr), col),
                   pl.BlockSpec((TOP_K, tr), col), pl.BlockSpec((tr, LANES), lambda i: (i, 0)),
                   pl.BlockSpec((N_EXPERTS, LANES), lambda i: (0, 0))],
        out_shape=[jax.ShapeDtypeStruct((TOP_K, t), jnp.int32),
                   jax.ShapeDtypeStruct((TOP_K, t), jnp.int32),
                   jax.ShapeDtypeStruct((t, LANES), F32),
                   jax.ShapeDtypeStruct((N_EXPERTS, LANES), F32)],
        scratch_shapes=[pltpu.VMEM((N_EXPERTS, LANES), F32)],
        compiler_params=_cparams(("arbitrary",)),
        name="moe_router",
    )(x2, wt, bias)


ISSUE_UNROLL = 8


def _pack_rows(x):
    lo = pltpu.bitcast(x[:, :X_WORDS].astype(BF16).astype(F32), jnp.uint32)
    hi = pltpu.bitcast(x[:, X_WORDS:].astype(BF16).astype(F32), jnp.uint32)
    return (lo >> 16) | (hi & jnp.uint32(0xFFFF0000))


def _unpack_words(words):
    return (pltpu.bitcast(words << 16, F32), pltpu.bitcast(words & jnp.uint32(0xFFFF0000), F32))


def _slots_kernel(start_ref, idx_ref, rank_ref, dest_ref):
    idx = idx_ref[...]
    dest = rank_ref[...]
    for e in range(N_EXPERTS):
        dest = dest + jnp.where(idx == e, start_ref[e], 0)
    dest_ref[...] = dest


def _slots(pad_start, idx, rank, tl):
    t = idx.shape[1]
    col = pl.BlockSpec((TOP_K, tl), lambda i: (0, i))
    return pl.pallas_call(
        _slots_kernel,
        grid=(t // tl,),
        in_specs=[pl.BlockSpec(memory_space=pltpu.SMEM), col, col],
        out_specs=col,
        out_shape=jax.ShapeDtypeStruct(idx.shape, jnp.int32),
        compiler_params=_cparams(("arbitrary",)),
        name="moe_slots",
    )(pad_start, idx, rank)


def _dispatch_kernel(dest_ref, x_ref, zeros_ref, xs_ref, xp_ref, sem, *, td):
    del zeros_ref
    words = _pack_rows(x_ref[...])
    for j in range(X_SLAB):
        xp_ref[:, j, :] = words[:, j * LANES:(j + 1) * LANES]

    def issue(g, carry):
        for u in range(ISSUE_UNROLL):
            t = g * ISSUE_UNROLL + u
            for j in range(TOP_K):
                pltpu.make_async_copy(xp_ref.at[t], xs_ref.at[dest_ref[j, t]], sem).start(priority=j % 2)
        return carry

    lax.fori_loop(0, td // ISSUE_UNROLL, issue, 0)
    for j in range(TOP_K):
        pltpu.make_async_copy(xp_ref, xs_ref.at[pl.ds(0, td)], sem).wait()


def _dispatch(dest, x2, zeros, td):
    t = x2.shape[0]
    return pl.pallas_call(
        functools.partial(_dispatch_kernel, td=td),
        grid=(t // td,),
        in_specs=[pl.BlockSpec((TOP_K, td), lambda i: (0, i), memory_space=pltpu.SMEM),
                  pl.BlockSpec((td, D_MODEL), lambda i: (i, 0)),
                  pl.BlockSpec(memory_space=pl.ANY)],
        out_specs=pl.BlockSpec(memory_space=pl.ANY),
        out_shape=jax.ShapeDtypeStruct(zeros.shape, zeros.dtype),
        input_output_aliases={2: 0},
        scratch_shapes=[pltpu.VMEM((td, X_SLAB, LANES), jnp.uint32), pltpu.SemaphoreType.DMA],
        compiler_params=_cparams(("arbitrary",)),
        name="moe_dispatch",
    )(dest, x2, zeros)


def _unpack_rows(words):
    return jnp.concatenate(_unpack_words(words), axis=1).astype(BF16)


def _swiglu(xb, w_gu, w_d):
    h = jnp.dot(xb, w_gu, preferred_element_type=F32)
    act = h[:, :EXPERT_FF] * jax.nn.sigmoid(h[:, :EXPERT_FF]) * h[:, EXPERT_FF:]
    return jnp.dot(act.astype(BF16), w_d, preferred_element_type=F32)


def _expert_kernel(blk_ref, exp_ref, used_ref, xs_ref, wgu_ref, wd_ref, ys_ref, wgu_b, wd_b, *, te):
    del blk_ref
    b = pl.program_id(0)
    live = b < used_ref[0]

    @pl.when(jnp.logical_or(b == 0, exp_ref[b] != exp_ref[jnp.maximum(b - 1, 0)]))
    def _():
        wgu_b[...] = wgu_ref[0, 0].astype(BF16)
        wd_b[...] = wd_ref[0, 0].astype(BF16)

    @pl.when(live)
    def _():
        words = jnp.concatenate(
            [xs_ref[pl.ds(j, te, stride=X_SLAB), :] for j in range(X_SLAB)], axis=1)
        y = _pack_rows(_swiglu(_unpack_rows(words), wgu_b[...], wd_b[...]))
        for c in range(X_SLAB):
            ys_ref[pl.ds(c, te, stride=X_SLAB), :] = y[:, c * LANES:(c + 1) * LANES]

    @pl.when(jnp.logical_not(live))
    def _():
        ys_ref[...] = jnp.zeros(ys_ref.shape, ys_ref.dtype)


def _experts(blk, exp, used, xs2, w_gu, w_d, layer, te):
    n_blocks = blk.shape[0]
    n_rows = xs2.shape[0] // X_SLAB
    return pl.pallas_call(
        functools.partial(_expert_kernel, te=te),
        grid_spec=pltpu.PrefetchScalarGridSpec(
            num_scalar_prefetch=3,
            grid=(n_blocks,),
            in_specs=[pl.BlockSpec((te * X_SLAB, LANES), lambda b, blk, exp, used: (blk[b], 0)),
                      pl.BlockSpec((1, 1, D_MODEL, 2 * EXPERT_FF),
                                   lambda b, blk, exp, used: (layer, exp[b], 0, 0)),
                      pl.BlockSpec((1, 1, EXPERT_FF, D_MODEL),
                                   lambda b, blk, exp, used: (layer, exp[b], 0, 0))],
            out_specs=pl.BlockSpec((te * X_SLAB, LANES), lambda b, blk, exp, used: (b, 0)),
            scratch_shapes=[pltpu.VMEM((D_MODEL, 2 * EXPERT_FF), BF16),
                            pltpu.VMEM((EXPERT_FF, D_MODEL), BF16)],
        ),
        out_shape=jax.ShapeDtypeStruct((n_rows * X_SLAB, LANES), jnp.uint32),
        compiler_params=_cparams(("arbitrary",)),
        name="moe_experts",
    )(blk, exp, used, xs2, w_gu, w_d)


def _combine_kernel(dest_ref, dest_next_ref, gate_ref, x_ref, wgu_ref, wd_ref, g_ref, b_ref, ys_ref, ys2_ref,
                    o_ref, ybuf_ref, sem, *, tc):
    i = pl.program_id(0)
    last = pl.num_programs(0) - 1
    slot = i % 2
    rows = TOP_K * tc

    def row_copy(d_ref, j, t, into):
        dst = ybuf_ref.at[pl.ds((into * rows + j * tc + t) * X_SLAB, X_SLAB)]
        return pltpu.make_async_copy(ys_ref.at[d_ref[j, t]], dst, sem.at[into])

    def wait_tile(into):
        pltpu.make_async_copy(ys2_ref.at[pl.ds(0, rows * X_SLAB)],
                              ybuf_ref.at[pl.ds(into * rows * X_SLAB, rows * X_SLAB)], sem.at[into]).wait()

    @pl.when(i == 0)
    def _():
        def issue(t, carry):
            for j in range(TOP_K):
                row_copy(dest_ref, j, t, 0).start(priority=j % 2)
            return carry
        lax.fori_loop(0, tc, issue, 0)

    wait_tile(slot)
    x = x_ref[...]
    gate = gate_ref[...]
    acc_lo = [None] * X_SLAB
    acc_hi = [None] * X_SLAB
    per_phase = tc // TOP_K
    for j in range(TOP_K):
        for t in range(j * per_phase, (j + 1) * per_phase):
            for jj in range(TOP_K):
                row_copy(dest_next_ref, jj, t, 1 - slot).start(priority=jj % 2)
        g_j = gate[:, j:j + 1]
        for c in range(X_SLAB):
            start = (slot * rows + j * tc) * X_SLAB + c
            lo, hi = _unpack_words(ybuf_ref[pl.ds(start, tc, stride=X_SLAB), :])
            acc_lo[c] = g_j * lo if j == 0 else acc_lo[c] + g_j * lo
            acc_hi[c] = g_j * hi if j == 0 else acc_hi[c] + g_j * hi
    routed = jnp.concatenate(acc_lo + acc_hi, axis=1)
    acc = ALPHA * x + routed + _swiglu(x.astype(BF16), wgu_ref[...], wd_ref[...])
    o_ref[...] = _layer_norm(acc, g_ref[...], b_ref[...])

    @pl.when(i == last)
    def _():
        wait_tile(1 - slot)


def _combine(dest, gate_tm, x2, ys, ws_gu, ws_d, g, b, tc):
    t = x2.shape[0]
    n = t // tc
    fixed = lambda i: (0, 0)
    row = lambda i: (i, 0)
    smem = lambda f: pl.BlockSpec((TOP_K, tc), f, memory_space=pltpu.SMEM)
    return pl.pallas_call(
        functools.partial(_combine_kernel, tc=tc),
        grid=(n,),
        in_specs=[smem(lambda i: (0, i)), smem(lambda i: (0, jnp.minimum(i + 1, n - 1))),
                  pl.BlockSpec((tc, LANES), row), pl.BlockSpec((tc, D_MODEL), row),
                  pl.BlockSpec((D_MODEL, 2 * EXPERT_FF), fixed), pl.BlockSpec((EXPERT_FF, D_MODEL), fixed),
                  pl.BlockSpec((1, D_MODEL), fixed), pl.BlockSpec((1, D_MODEL), fixed),
                  pl.BlockSpec(memory_space=pl.ANY), pl.BlockSpec(memory_space=pl.ANY)],
        out_specs=pl.BlockSpec((tc, D_MODEL), row),
        out_shape=jax.ShapeDtypeStruct((t, D_MODEL), F32),
        scratch_shapes=[pltpu.VMEM((2 * TOP_K * tc * X_SLAB, LANES), jnp.uint32),
                        pltpu.SemaphoreType.DMA((2,))],
        compiler_params=_cparams(("arbitrary",)),
        name="moe_combine",
    )(dest, dest, gate_tm, x2, ws_gu, ws_d, g, b, ys.reshape(-1, X_SLAB, LANES), ys)


def _moe_layer(x2, w_router, bias, w_gu, w_d, layer, ws_gu, ws_d, ln_g, ln_b, sorted_buf=None,
               te=1024, tr=512, td=512, tc=128):
    t = x2.shape[0]
    n_blocks = t * TOP_K // te + N_EXPERTS
    n_rows = n_blocks * te
    if sorted_buf is None:
        sorted_buf = jnp.zeros((n_rows, X_SLAB, LANES), jnp.uint32)
    idx, rank, gate_tm, cnt = _router(x2, w_router.T.astype(F32), bias.reshape(-1, 1).astype(F32), tr)

    counts = cnt[:, 0].astype(jnp.int32)
    padded = (counts + te - 1) // te * te
    pad_end = jnp.cumsum(padded)
    pad_start = pad_end - padded
    dest = _slots(pad_start, idx, rank, min(t, 4096))
    used = pad_end[-1] // te
    blk = jnp.minimum(jnp.arange(n_blocks, dtype=jnp.int32), used - 1)
    exp = jnp.sum((pad_end[None, :] // te <= blk[:, None]).astype(jnp.int32), axis=1)
    exp = jnp.minimum(exp, N_EXPERTS - 1)

    xs = _dispatch(dest, x2, sorted_buf, td)
    ys = _experts(blk, exp.astype(jnp.int32), used.reshape(1), xs.reshape(n_rows * X_SLAB, LANES),
                  w_gu, w_d, layer, te)
    out = _combine(dest, gate_tm, x2, ys, ws_gu.astype(BF16), ws_d.astype(BF16),
                   ln_g.reshape(1, -1), ln_b.reshape(1, -1), tc)
    return out, xs


def kernel(x, attn_w_in, attn_w_out, attn_lambda, attn_subln, hgrn_w_in, hgrn_w_out, hgrn_lower_bounds, hgrn_norm, moe_router, moe_router_bias, moe_w_gate_up, moe_w_down, shared_w_gate_up, shared_w_down, ln_gain, ln_bias):
    bsz, seq, _ = x.shape
    tables = _rotary_tables(seq)
    lb = jax.nn.softmax(hgrn_lower_bounds.astype(F32), axis=0)
    lb = jnp.cumsum(lb, axis=0) - lb[0]
    x2 = x.reshape(bsz * seq, D_MODEL)
    sorted_buf = None
    for layer in range(DEPTH):
        j = layer // 2
        if layer % 2 == 0:
            lambda_init = 0.8 - 0.6 * math.exp(-0.3 * layer)
            x2 = _attn_layer(x2, bsz, seq, attn_w_in[j], attn_w_out[j], attn_lambda[j], attn_subln[j],
                             lambda_init, ln_gain[layer, 0], ln_bias[layer, 0], tables)
        else:
            x2 = _hgrn_layer(x2, bsz, seq, hgrn_w_in[j], hgrn_w_out[j], lb[layer], hgrn_norm[j],
                             ln_gain[layer, 0], ln_bias[layer, 0])
        x2, sorted_buf = _moe_layer(x2, moe_router[layer], moe_router_bias[layer], moe_w_gate_up,
                                    moe_w_down, layer, shared_w_gate_up[layer], shared_w_down[layer],
                                    ln_gain[layer, 1], ln_bias[layer, 1], sorted_buf)
    return x2.reshape(bsz, seq, D_MODEL)
```
